```python
import math
import jax
import jax.numpy as jnp
from jax import lax
import numpy as np


D_MODEL = 2048
BATCH = 4
SEQ = 8192
DEPTH = 2

S5_GROUP = 16
S5_GROUPS = D_MODEL // S5_GROUP
S5_STATE = 64
S5_CHUNK = 128
N_HEADS = 16
HEAD_DIM = D_MODEL // N_HEADS
Q_BLOCK = 128
D_FF = 7 * D_MODEL // 2
N_EXPERTS = 8
TOP_K = 2
TOKEN_BLOCK = 128
N_A_LAYERS = DEPTH // 2
N_B_LAYERS = DEPTH - N_A_LAYERS
N_DENSE_LAYERS = (DEPTH + 1) // 2
N_MOE_LAYERS = DEPTH // 2
RMS_EPS = 1e-6
ADA_INIT = 0.5

kernel_name = "yoco_s5_stickbreak_moe_block"


def rms_norm(x, g):
    xf = x.astype(jnp.float32)
    y = xf * lax.rsqrt(jnp.mean(xf * xf, axis=-1, keepdims=True) + RMS_EPS)
    return (y * g.astype(jnp.float32)).astype(x.dtype)


def ada_mods(c, w, b, n):
    mods = jax.nn.silu(c.astype(jnp.float32)) @ w.astype(jnp.float32) + b.astype(jnp.float32)
    return jnp.split(mods[:, None, :], n, axis=-1)


def modulate(h, shift, scale):
    return (h.astype(jnp.float32) * (1.0 + scale) + shift).astype(h.dtype)


def to_blocks(t, blk):
    b, s = t.shape[0], t.shape[1]
    return t.reshape(b, s // blk, blk, *t.shape[2:]).swapaxes(0, 1)


def from_blocks(t):
    nb, b, blk = t.shape[0], t.shape[1], t.shape[2]
    return t.swapaxes(0, 1).reshape(b, nb * blk, *t.shape[3:])


def s5_mixer(u, a_re, a_im, b_re, b_im, c_re, c_im, d_skip, log_step, glu_w, glu_b):
    bsz, s, dm = u.shape
    f32 = jnp.float32
    uf = u.astype(f32)
    lam = lax.complex(a_re.astype(f32), a_im.astype(f32))
    dt = jnp.exp(log_step.astype(f32))[:, None]
    lam_dt = lam * dt
    lam_bar = jnp.exp(lam_dt)
    b_c = lax.complex(b_re.astype(f32), b_im.astype(f32))
    b_bar = ((lam_bar - 1.0) / lam)[..., None] * b_c
    c_c = lax.complex(c_re.astype(f32), c_im.astype(f32))
    steps = jnp.arange(1, S5_CHUNK + 1, dtype=f32)
    powers = jnp.exp(lam_dt[None] * steps[:, None, None])
    u_ch = to_blocks(uf.reshape(bsz, s, S5_GROUPS, S5_GROUP), S5_CHUNK)

    def combine(e1, e2):
        a1, x1 = e1
        a2, x2 = e2
        return a2 * a1, a2 * x1 + x2

    def chunk_step(carry, u_c):
        bu = jnp.einsum('blgh,gph->blgp', u_c.astype(jnp.complex64), b_bar)
        a = jnp.broadcast_to(lam_bar, bu.shape)
        _, s_loc = lax.associative_scan(combine, (a, bu), axis=1)
        st = s_loc + powers[None] * carry[:, None]
        y = jnp.einsum('blgp,ghp->blgh', st, c_c).real
        return st[:, -1], y

    carry0 = jnp.zeros((bsz, S5_GROUPS, S5_STATE), jnp.complex64)
    _, y = lax.scan(chunk_step, carry0, u_ch)
    y = from_blocks(y).reshape(bsz, s, dm) + d_skip.astype(f32) * uf
    y = jax.nn.gelu(y)
    out = y * jax.nn.sigmoid(y @ glu_w.astype(f32) + glu_b.astype(f32))
    return out.astype(u.dtype)


def stick_breaking_attention(q, k, v):
    bsz, nh, s, hd = q.shape
    nblk = s // Q_BLOCK
    scale = 1.0 / math.sqrt(hd)
    kf = k.astype(jnp.float32)
    vf = v.astype(jnp.float32)
    qb = q.reshape(bsz, nh, nblk, Q_BLOCK, hd).transpose(2, 0, 1, 3, 4)
    key_pos = jnp.arange(s)

    def block(args):
        q_blk, i = args
        z = jnp.einsum('bhqd,bhkd->bhqk', q_blk.astype(jnp.float32), kf) * scale
        q_pos = i * Q_BLOCK + jnp.arange(Q_BLOCK)
        mask = key_pos[None, :] < q_pos[:, None]
        log_beta = jax.nn.log_sigmoid(z)
        log_1m = jnp.where(mask, jax.nn.log_sigmoid(-z), 0.0)
        suffix = lax.cumsum(log_1m, axis=3, reverse=True) - log_1m
        w = jnp.where(mask, jnp.exp(log_beta + suffix), 0.0)
        return jnp.einsum('bhqk,bhkd->bhqd', w, vf)

    out = lax.map(block, (qb, jnp.arange(nblk)))
    return out.transpose(1, 2, 0, 3, 4).reshape(bsz, nh, s, hd).astype(q.dtype)


def swiglu_ffn(h, w_gate, w_up, w_down):
    def blk(xb):
        return (jax.nn.silu(xb @ w_gate) * (xb @ w_up)) @ w_down
    return from_blocks(lax.map(blk, to_blocks(h, TOKEN_BLOCK)))


def moe_swiglu(h, w_router, w_gate, w_up, w_down):
    logits = jnp.einsum('bsd,de->bse', h.astype(jnp.float32), w_router.astype(jnp.float32))
    top_vals, top_idx = lax.top_k(logits, TOP_K)
    probs = jax.nn.softmax(top_vals, axis=-1)
    gates = jnp.sum(jax.nn.one_hot(top_idx, N_EXPERTS, dtype=jnp.float32) * probs[..., None], axis=-2)

    def blk(args):
        xb, gb = args
        a = jnp.einsum('btd,edf->btef', xb, w_gate)
        u = jnp.einsum('btd,edf->btef', xb, w_up)
        hid = jax.nn.silu(a) * u * gb[..., None].astype(a.dtype)
        return jnp.einsum('btef,efd->btd', hid, w_down)

    out = lax.map(blk, (to_blocks(h, TOKEN_BLOCK), to_blocks(gates, TOKEN_BLOCK)))
    return from_blocks(out).astype(h.dtype)


def setup_inputs(seed: int = 0) -> dict:
    key = jax.random.key(seed)
    ks = iter(jax.random.split(key, 48))
    f32 = jnp.float32
    D, G, P, H = D_MODEL, S5_GROUPS, S5_STATE, S5_GROUP

    def nrm(shape, fan_in, scale=1.0):
        return jax.random.normal(next(ks), shape, f32) * (scale * fan_in ** -0.5)

    def gain(shape):
        return 1.0 + 0.05 * jax.random.normal(next(ks), shape, f32)

    def bias(shape):
        return 0.01 * jax.random.normal(next(ks), shape, f32)

    x = jax.random.normal(next(ks), (BATCH, SEQ, D), f32)
    c = jax.random.normal(next(ks), (BATCH, D), f32)
    inputs = {
        'x': x,
        'c': c,
        'ada_mix_w': nrm((DEPTH, D, 3 * D), D, ADA_INIT),
        'ada_mix_b': bias((DEPTH, 3 * D)),
        'norm_mix_pre': gain((DEPTH, D)),
        'norm_mix_post': gain((DEPTH, D)),
        'ada_ffn_w': nrm((DEPTH, D, 3 * D), D, ADA_INIT),
        'ada_ffn_b': bias((DEPTH, 3 * D)),
        'norm_ffn_pre': gain((DEPTH, D)),
        'norm_ffn_post': gain((DEPTH, D)),
        's5_a_re': -0.5 + 0.01 * jax.random.normal(next(ks), (N_A_LAYERS, G, P), f32),
        's5_a_im': jnp.pi * jnp.arange(P, dtype=f32)[None, None, :]
                   + 0.01 * jax.random.normal(next(ks), (N_A_LAYERS, G, P), f32),
        's5_b_re': nrm((N_A_LAYERS, G, P, H), 2 * H),
        's5_b_im': nrm((N_A_LAYERS, G, P, H), 2 * H),
        's5_c_re': nrm((N_A_LAYERS, G, H, P), 2 * P),
        's5_c_im': nrm((N_A_LAYERS, G, H, P), 2 * P),
        's5_d': jax.random.normal(next(ks), (N_A_LAYERS, D), f32),
        's5_log_step': jax.random.uniform(next(ks), (N_A_LAYERS, G), f32,
                                          math.log(1e-3), math.log(1e-1)),
        's5_glu_w': nrm((N_A_LAYERS, D, D), D),
        's5_glu_b': bias((N_A_LAYERS, D)),
        'kv_ada_w': nrm((D, 2 * D), D, ADA_INIT),
        'kv_ada_b': bias((2 * D,)),
        'kv_norm': gain((D,)),
        'w_k': nrm((D, D), D),
        'w_v': nrm((D, D), D),
        'w_q': nrm((N_B_LAYERS, D, D), D),
        'w_o': nrm((N_B_LAYERS, D, D), D),
        'ffn_w_gate': nrm((N_DENSE_LAYERS, D, D_FF), D),
        'ffn_w_up': nrm((N_DENSE_LAYERS, D, D_FF), D),
        'ffn_w_down': nrm((N_DENSE_LAYERS, D_FF, D), D_FF),
        'moe_w_router': nrm((N_MOE_LAYERS, D, N_EXPERTS), D),
        'moe_w_gate': nrm((N_MOE_LAYERS, N_EXPERTS, D, D_FF), D),
        'moe_w_up': nrm((N_MOE_LAYERS, N_EXPERTS, D, D_FF), D),
        'moe_w_down': nrm((N_MOE_LAYERS, N_EXPERTS, D_FF, D), D_FF),
    }
    return inputs


def reference(x, c, ada_mix_w, ada_mix_b, norm_mix_pre, norm_mix_post,
              ada_ffn_w, ada_ffn_b, norm_ffn_pre, norm_ffn_post,
              s5_a_re, s5_a_im, s5_b_re, s5_b_im, s5_c_re, s5_c_im, s5_d, s5_log_step,
              s5_glu_w, s5_glu_b,
              kv_ada_w, kv_ada_b, kv_norm, w_k, w_v, w_q, w_o,
              ffn_w_gate, ffn_w_up, ffn_w_down,
              moe_w_router, moe_w_gate, moe_w_up, moe_w_down):
    bsz, s, dm = x.shape
    k_sh = None
    v_sh = None
    for layer in range(DEPTH):
        shift, scale, gate = ada_mods(c, ada_mix_w[layer], ada_mix_b[layer], 3)
        h = modulate(rms_norm(x, norm_mix_pre[layer]), shift, scale)
        if layer < N_A_LAYERS:
            i = layer
            m = s5_mixer(h, s5_a_re[i], s5_a_im[i], s5_b_re[i], s5_b_im[i],
                         s5_c_re[i], s5_c_im[i], s5_d[i], s5_log_step[i],
                         s5_glu_w[i], s5_glu_b[i])
        else:
            j = layer - N_A_LAYERS
            if k_sh is None:
                kv_shift, kv_scale = ada_mods(c, kv_ada_w, kv_ada_b, 2)
                kv_in = modulate(rms_norm(x, kv_norm), kv_shift, kv_scale)
                k_sh = (kv_in @ w_k).reshape(bsz, s, N_HEADS, HEAD_DIM).transpose(0, 2, 1, 3)
                v_sh = (kv_in @ w_v).reshape(bsz, s, N_HEADS, HEAD_DIM).transpose(0, 2, 1, 3)
            q = (h @ w_q[j]).reshape(bsz, s, N_HEADS, HEAD_DIM).transpose(0, 2, 1, 3)
            o = stick_breaking_attention(q, k_sh, v_sh)
            m = o.transpose(0, 2, 1, 3).reshape(bsz, s, dm) @ w_o[j]
        post = rms_norm(m, norm_mix_post[layer]).astype(jnp.float32)
        x = x + ((1.0 + gate) * post).astype(x.dtype)

        shift, scale, gate = ada_mods(c, ada_ffn_w[layer], ada_ffn_b[layer], 3)
        h = modulate(rms_norm(x, norm_ffn_pre[layer]), shift, scale)
        if layer % 2 == 0:
            e = layer // 2
            f = swiglu_ffn(h, ffn_w_gate[e], ffn_w_up[e], ffn_w_down[e])
        else:
            e = layer // 2
            f = moe_swiglu(h, moe_w_router[e], moe_w_gate[e], moe_w_up[e], moe_w_down[e])
        post = rms_norm(f, norm_ffn_post[layer]).astype(jnp.float32)
        x = x + ((1.0 + gate) * post).astype(x.dtype)
    return x
```

```python
import functools
import math

import jax
import jax.numpy as jnp
from jax import lax
from jax.experimental import pallas as pl
from jax.experimental.pallas import tpu as pltpu

F32 = jnp.float32
BF16 = jnp.bfloat16

LANES = 128
VMEM_LIMIT = 56 * 1024 * 1024
RMS_EPS = 1e-6

S5_GROUP = 16
S5_STATE = 64
CHUNK = 16
OCT = LANES // S5_GROUP
OCT_STATE = 2 * OCT * S5_STATE
N_HEADS = 16
N_EXPERTS = 8
HIGHEST = lax.Precision.HIGHEST


def _cparams(*semantics):
    return pltpu.CompilerParams(dimension_semantics=semantics, vmem_limit_bytes=VMEM_LIMIT)


def _dot(a, b):
    return jnp.dot(a, b, preferred_element_type=F32)


def _rms(xf):
    ms = jnp.mean(xf * xf, axis=-1, keepdims=True)
    return xf * lax.rsqrt(ms + RMS_EPS)


def _norm_mod(xf, g, shift, scale):
    return _rms(xf) * (g * (1.0 + scale)) + shift


def _ada_kernel(c_ref, w_ref, b_ref, o_ref):
    s = c_ref[...]
    s = s * jax.nn.sigmoid(s)
    o_ref[...] = _dot(s.astype(BF16), w_ref[...].astype(BF16)) + b_ref[...]


def _ada(c_pad, w, b, tn=1024):
    nl, d, n = w.shape
    return pl.pallas_call(
        _ada_kernel,
        grid=(nl, n // tn),
        in_specs=[pl.BlockSpec((8, d), lambda l, j: (0, 0)),
                  pl.BlockSpec((None, d, tn), lambda l, j: (l, 0, j)),
                  pl.BlockSpec((None, 1, tn), lambda l, j: (l, 0, j))],
        out_specs=pl.BlockSpec((None, 8, tn), lambda l, j: (l, 0, j)),
        out_shape=jax.ShapeDtypeStruct((nl, 8, n), F32),
        compiler_params=_cparams("parallel", "parallel"),
        name="ada_mods",
    )(c_pad, w, b.reshape(nl, 1, n))


def _s5_tables(a_re, a_im, b_re, b_im, c_re, c_im, log_step, d_skip, batch):
    g, p = a_re.shape
    h, lc = S5_GROUP, CHUNK
    no = g // OCT
    dt = jnp.exp(log_step)[:, None]
    lr, li = a_re * dt, a_im * dt
    k = jnp.arange(lc + 1, dtype=F32)[:, None, None]
    mag = jnp.exp(k * lr[None])
    pw_r, pw_i = mag * jnp.cos(k * li[None]), mag * jnp.sin(k * li[None])
    nr, ni = pw_r[1] - 1.0, pw_i[1]
    den = a_re * a_re + a_im * a_im
    qr, qi = (nr * a_re + ni * a_im) / den, (ni * a_re - nr * a_im) / den
    bbr = qr[..., None] * b_re - qi[..., None] * b_im
    bbi = qr[..., None] * b_im + qi[..., None] * b_re
    eye = jnp.eye(OCT, dtype=F32)

    wr, wi = pw_r[:lc, :, None, :], pw_i[:lc, :, None, :]
    cwr = c_re[None] * wr - c_im[None] * wi
    cwi = c_re[None] * wi + c_im[None] * wr
    kk = (jnp.einsum('tgop,gpi->tgio', cwr, bbr, precision=HIGHEST)
          - jnp.einsum('tgop,gpi->tgio', cwi, bbi, precision=HIGHEST))
    s_idx = jnp.arange(lc)[:, None]
    t_idx = jnp.arange(lc)[None, :]
    ksel = jnp.where((t_idx >= s_idx)[:, :, None, None, None],
                     kk[jnp.clip(t_idx - s_idx, 0, lc - 1)], 0.0)
    ksel = ksel.reshape(lc, lc, no, OCT, h, h)
    m1 = jnp.einsum('stoghk,gq->osghtqk', ksel, eye).reshape(no, lc * LANES, lc * LANES)

    er = c_re[None] * pw_r[1:, :, None, :] - c_im[None] * pw_i[1:, :, None, :]
    ei = c_re[None] * pw_i[1:, :, None, :] + c_im[None] * pw_r[1:, :, None, :]
    e2 = jnp.stack([er, -ei]).reshape(2, lc, no, OCT, h, p)
    m2 = jnp.einsum('ctoghp,gq->ocgptqh', e2, eye).reshape(no, OCT_STATE, lc * LANES)

    fr = pw_r[lc - 1::-1][:lc, :, :, None] * bbr[None] - pw_i[lc - 1::-1][:lc, :, :, None] * bbi[None]
    fi = pw_r[lc - 1::-1][:lc, :, :, None] * bbi[None] + pw_i[lc - 1::-1][:lc, :, :, None] * bbr[None]
    f2 = jnp.stack([fr, fi]).reshape(2, lc, no, OCT, p, h)
    m3 = jnp.einsum('csogph,gq->osghcqp', f2, eye).reshape(no, lc * LANES, OCT_STATE)

    ar = pw_r[lc].reshape(no, OCT * p)
    ai = pw_i[lc].reshape(no, OCT * p)
    a1 = jnp.tile(jnp.concatenate([ar, ar], axis=-1), (1, batch))[:, None, :]
    a2 = jnp.tile(jnp.concatenate([-ai, ai], axis=-1), (1, batch))[:, None, :]
    dsk = jnp.tile(d_skip.reshape(no, 1, LANES), (1, 1, lc))
    return m1.astype(BF16), m2.astype(BF16), m3.astype(BF16), a1, a2, dsk


def _s5_pre_kernel(x_ref, g_ref, sh_ref, sc_ref, o_ref, slab):
    tm = x_ref.shape[0]
    nt = x_ref.shape[1] // LANES
    rows = tm // CHUNK
    h = _norm_mod(x_ref[...], g_ref[...], sh_ref[...], sc_ref[...])
    for j in range(nt):
        slab[j] = h[:, j * LANES:(j + 1) * LANES]
    for j in range(nt):
        for t in range(CHUNK):
            o_ref[j, :, t * LANES:(t + 1) * LANES] = slab.at[j][pl.ds(t, rows, stride=CHUNK), :].astype(BF16)


def _s5_pre(x2, g, shift, scale, seq, tm=512):
    t, d = x2.shape
    nt = d // LANES
    per_b = seq // tm
    return pl.pallas_call(
        _s5_pre_kernel,
        grid=(t // tm,),
        in_specs=[pl.BlockSpec((tm, d), lambda i: (i, 0)),
                  pl.BlockSpec((1, d), lambda i: (0, 0)),
                  pl.BlockSpec((None, 1, d), lambda i: (i // per_b, 0, 0)),
                  pl.BlockSpec((None, 1, d), lambda i: (i // per_b, 0, 0))],
        out_specs=pl.BlockSpec((nt, tm // CHUNK, CHUNK * LANES), lambda i: (0, i, 0)),
        out_shape=jax.ShapeDtypeStruct((nt, t // CHUNK, CHUNK * LANES), BF16),
        scratch_shapes=[pltpu.VMEM((nt, tm, LANES), F32)],
        compiler_params=_cparams("parallel"),
        name="s5_pre",
    )(x2, g, shift, scale)


def _s5_state_kernel(h_ref, m3_ref, o_ref):
    o_ref[...] = _dot(h_ref[...], m3_ref[...])


def _s5_state(ho, m3, batch):
    no, r, kc = ho.shape
    c = r // batch
    return pl.pallas_call(
        _s5_state_kernel,
        grid=(no, batch),
        in_specs=[pl.BlockSpec((None, c, kc), lambda j, b: (j, b, 0)),
                  pl.BlockSpec((None, kc, OCT_STATE), lambda j, b: (j, 0, 0))],
        out_specs=pl.BlockSpec((None, c, OCT_STATE), lambda j, b: (j, 0, b)),
        out_shape=jax.ShapeDtypeStruct((no, c, batch * OCT_STATE), F32),
        compiler_params=_cparams("parallel", "parallel"),
        name="s5_state",
    )(ho, m3)


def _s5_scan_kernel(v_ref, a1_ref, a2_ref, o_ref, *, batch, unroll):
    a1 = a1_ref[...]
    a2 = a2_ref[...]
    half = OCT_STATE // 2

    def swap(s):
        parts = []
        for b in range(batch):
            parts.append(s[:, b * OCT_STATE + half:(b + 1) * OCT_STATE])
            parts.append(s[:, b * OCT_STATE:b * OCT_STATE + half])
        return jnp.concatenate(parts, axis=-1)

    def body(i, s):
        base = pl.multiple_of(i * unroll, unroll)
        v = v_ref[pl.ds(base, unroll), :]
        rows = []
        for k in range(unroll):
            rows.append(s)
            s = a1 * s + a2 * swap(s) + v[k:k + 1, :]
        o_ref[pl.ds(base, unroll), :] = jnp.concatenate(rows, axis=0).astype(o_ref.dtype)
        return s

    lax.fori_loop(0, v_ref.shape[0] // unroll, body, jnp.zeros((1, v_ref.shape[1]), F32))


def _s5_scan(v, a1, a2, batch, unroll=16):
    no, c, w = v.shape
    return pl.pallas_call(
        functools.partial(_s5_scan_kernel, batch=batch, unroll=unroll),
        grid=(no,),
        in_specs=[pl.BlockSpec((None, c, w), lambda j: (j, 0, 0)),
                  pl.BlockSpec((None, 1, w), lambda j: (j, 0, 0)),
                  pl.BlockSpec((None, 1, w), lambda j: (j, 0, 0))],
        out_specs=pl.BlockSpec((None, c, w), lambda j: (j, 0, 0)),
        out_shape=jax.ShapeDtypeStruct((no, c, w), BF16),
        compiler_params=_cparams("parallel"),
        name="s5_scan",
    )(v, a1, a2)


def _s5_out_kernel(h_ref, s_ref, m1_ref, m2_ref, d_ref, o_ref):
    h = h_ref[...]
    y = _dot(h, m1_ref[...]) + _dot(s_ref[...], m2_ref[...]) + d_ref[...] * h.astype(F32)
    o_ref[...] = jax.nn.gelu(y).astype(BF16)


def _s5_out(ho, sin, m1, m2, dsk, batch):
    no, r, kc = ho.shape
    c = r // batch
    return pl.pallas_call(
        _s5_out_kernel,
        grid=(no, batch),
        in_specs=[pl.BlockSpec((None, c, kc), lambda j, b: (j, b, 0)),
                  pl.BlockSpec((None, c, OCT_STATE), lambda j, b: (j, 0, b)),
                  pl.BlockSpec((None, kc, kc), lambda j, b: (j, 0, 0)),
                  pl.BlockSpec((None, OCT_STATE, kc), lambda j, b: (j, 0, 0)),
                  pl.BlockSpec((None, 1, kc), lambda j, b: (j, 0, 0))],
        out_specs=pl.BlockSpec((None, c, kc), lambda j, b: (j, b, 0)),
        out_shape=jax.ShapeDtypeStruct((no, r, kc), BF16),
        compiler_params=_cparams("parallel", "parallel"),
        name="s5_out",
    )(ho, sin, m1, m2, dsk)


def _s5_glu_kernel(y_ref, x_ref, w_ref, b_ref, g_ref, gate_ref, o_ref, ybuf, slab):
    nt, rows, _ = y_ref.shape
    for t in range(CHUNK):
        for j in range(nt):
            ybuf[t * rows:(t + 1) * rows, j * LANES:(j + 1) * LANES] = y_ref[j, :, t * LANES:(t + 1) * LANES]
    y = ybuf[...]
    z = _dot(y, w_ref[...]) + b_ref[...]
    m = y.astype(F32) * jax.nn.sigmoid(z)
    res = (1.0 + gate_ref[...]) * (_rms(m) * g_ref[...])
    for j in range(nt):
        for t in range(CHUNK):
            slab.at[j][pl.ds(t, rows, stride=CHUNK), :] = res[t * rows:(t + 1) * rows, j * LANES:(j + 1) * LANES]
    for j in range(nt):
        o_ref[:, j * LANES:(j + 1) * LANES] = x_ref[:, j * LANES:(j + 1) * LANES] + slab[j]


def _s5_glu(yo, x2, w, b, g, gate, seq, tm=512):
    t, d = x2.shape
    nt = d // LANES
    per_b = seq // tm
    return pl.pallas_call(
        _s5_glu_kernel,
        grid=(t // tm,),
        in_specs=[pl.BlockSpec((nt, tm // CHUNK, CHUNK * LANES), lambda i: (0, i, 0)),
                  pl.BlockSpec((tm, d), lambda i: (i, 0)),
                  pl.BlockSpec((d, d), lambda i: (0, 0)),
                  pl.BlockSpec((1, d), lambda i: (0, 0)),
                  pl.BlockSpec((1, d), lambda i: (0, 0)),
                  pl.BlockSpec((None, 1, d), lambda i: (i // per_b, 0, 0))],
        out_specs=pl.BlockSpec((tm, d), lambda i: (i, 0)),
        out_shape=jax.ShapeDtypeStruct((t, d), F32),
        scratch_shapes=[pltpu.VMEM((tm, d), BF16), pltpu.VMEM((nt, tm, LANES), F32)],
        compiler_params=_cparams("parallel"),
        name="s5_glu",
    )(yo, x2, w, b, g, gate)


def _ffn_kernel(x_ref, g_ref, sh_ref, sc_ref, wg_ref, wu_ref, wd_ref, gp_ref, gate_ref, o_ref, h_scr, acc):
    j = pl.program_id(1)

    @pl.when(j == 0)
    def _():
        h_scr[...] = _norm_mod(x_ref[...], g_ref[...], sh_ref[...], sc_ref[...]).astype(BF16)
        acc[...] = jnp.zeros_like(acc)

    h = h_scr[...]
    a = _dot(h, wg_ref[...])
    u = _dot(h, wu_ref[...])
    hid = (a * jax.nn.sigmoid(a) * u).astype(BF16)
    acc[...] += _dot(hid, wd_ref[...])

    @pl.when(j == pl.num_programs(1) - 1)
    def _():
        o_ref[...] = x_ref[...] + (1.0 + gate_ref[...]) * (_rms(acc[...]) * gp_ref[...])


def _ffn(x2, g, shift, scale, wg, wu, wd, gp, gate, seq, tm=512, tf=512):
    t, d = x2.shape
    f = wg.shape[1]
    per_b = seq // tm
    mod = pl.BlockSpec((None, 1, d), lambda i, j: (i // per_b, 0, 0))
    row = pl.BlockSpec((1, d), lambda i, j: (0, 0))
    return pl.pallas_call(
        _ffn_kernel,
        grid=(t // tm, f // tf),
        in_specs=[pl.BlockSpec((tm, d), lambda i, j: (i, 0)), row, mod, mod,
                  pl.BlockSpec((d, tf), lambda i, j: (0, j)),
                  pl.BlockSpec((d, tf), lambda i, j: (0, j)),
                  pl.BlockSpec((tf, d), lambda i, j: (j, 0)),
                  row, mod],
        out_specs=pl.BlockSpec((tm, d), lambda i, j: (i, 0)),
        out_shape=jax.ShapeDtypeStruct((t, d), F32),
        scratch_shapes=[pltpu.VMEM((tm, d), BF16), pltpu.VMEM((tm, d), F32)],
        compiler_params=_cparams("parallel", "arbitrary"),
        name="dense_ffn",
    )(x2, g, shift, scale, wg, wu, wd, gp, gate)


def _qkv_kernel(x_ref, gq_ref, shq_ref, scq_ref, gkv_ref, shkv_ref, sckv_ref, w_ref, o_ref, hq, hkv, *, qscale):
    n = pl.program_id(1)

    @pl.when(n == 0)
    def _():
        xf = x_ref[...]
        xn = _rms(xf)
        hq[...] = (xn * (gq_ref[...] * (1.0 + scq_ref[...])) + shq_ref[...]).astype(BF16)
        hkv[...] = (xn * (gkv_ref[...] * (1.0 + sckv_ref[...])) + shkv_ref[...]).astype(BF16)
        o_ref[...] = (_dot(hq[...], w_ref[...]) * qscale).astype(BF16)

    @pl.when(n > 0)
    def _():
        o_ref[...] = _dot(hkv[...], w_ref[...]).astype(BF16)


def _qkv(x2, gq, shq, scq, gkv, shkv, sckv, w3, seq, tm=512):
    t, d = x2.shape
    per_b = seq // tm
    mod = pl.BlockSpec((None, 1, d), lambda i, n: (i // per_b, 0, 0))
    row = pl.BlockSpec((1, d), lambda i, n: (0, 0))
    qscale = 1.0 / math.sqrt(d // N_HEADS)
    return pl.pallas_call(
        functools.partial(_qkv_kernel, qscale=qscale),
        grid=(t // tm, 3),
        in_specs=[pl.BlockSpec((tm, d), lambda i, n: (i, 0)), row, mod, mod, row, mod, mod,
                  pl.BlockSpec((None, d, d), lambda i, n: (n, 0, 0))],
        out_specs=pl.BlockSpec((None, tm, d), lambda i, n: (n, i, 0)),
        out_shape=jax.ShapeDtypeStruct((3, t, d), BF16),
        scratch_shapes=[pltpu.VMEM((tm, d), BF16), pltpu.VMEM((tm, d), BF16)],
        compiler_params=_cparams("parallel", "arbitrary"),
        name="qkv_proj",
    )(x2, gq, shq, scq, gkv, shkv, sckv, w3)


def _attn_kernel(q_ref, k_ref, v_ref, o_ref, *, tb):
    qi = pl.program_id(2)
    q = q_ref[...]
    row = lax.broadcasted_iota(jnp.int32, (tb, tb), 0)
    col = lax.broadcasted_iota(jnp.int32, (tb, tb), 1)
    later = (row > col).astype(BF16)
    valid = col < row

    def block(kb, carry, diagonal):
        acc, run = carry
        start = pl.multiple_of(kb * tb, tb)
        k = k_ref[pl.ds(start, tb), :]
        v = v_ref[pl.ds(start, tb), :]
        z = lax.dot_general(q, k, (((1,), (1,)), ((), ())), preferred_element_type=F32)
        sp = jnp.maximum(z, 0.0) + jnp.log(1.0 + jnp.exp(-jnp.abs(z)))
        l1m = -sp
        if diagonal:
            l1m = jnp.where(valid, l1m, 0.0)
        hi = l1m.astype(BF16)
        lo = (l1m - hi.astype(F32)).astype(BF16)
        suffix = _dot(hi, later) + _dot(lo, later)
        w = jnp.exp((z - sp) + suffix + run)
        if diagonal:
            w = jnp.where(valid, w, 0.0)
        acc = acc + _dot(w.astype(BF16), v)
        run = run + jnp.sum(l1m, axis=-1, keepdims=True)
        return acc, run

    carry = (jnp.zeros((tb, q.shape[1]), F32), jnp.zeros((tb, 1), F32))
    carry = block(qi, carry, True)
    carry = lax.fori_loop(0, qi, lambda i, c: block(qi - 1 - i, c, False), carry)
    o_ref[...] = carry[0].astype(o_ref.dtype)


def _attention(qkv, batch, seq, tb=256):
    _, t, d = qkv.shape
    hd = d // N_HEADS
    nq = seq // tb
    return pl.pallas_call(
        functools.partial(_attn_kernel, tb=tb),
        grid=(batch, N_HEADS, nq),
        in_specs=[pl.BlockSpec((None, tb, hd), lambda b, h, i: (0, b * nq + i, h)),
                  pl.BlockSpec((None, seq, hd), lambda b, h, i: (1, b, h)),
                  pl.BlockSpec((None, seq, hd), lambda b, h, i: (2, b, h))],
        out_specs=pl.BlockSpec((tb, hd), lambda b, h, i: (b * nq + i, h)),
        out_shape=jax.ShapeDtypeStruct((t, d), BF16),
        compiler_params=_cparams("parallel", "parallel", "parallel"),
        name="stickbreak_attn",
    )(qkv, qkv, qkv)


def _oproj_kernel(a_ref, x_ref, w_ref, g_ref, gate_ref, o_ref):
    m = _dot(a_ref[...], w_ref[...])
    o_ref[...] = x_ref[...] + (1.0 + gate_ref[...]) * (_rms(m) * g_ref[...])


def _oproj(a, x2, w, g, gate, seq, tm=512):
    t, d = x2.shape
    per_b = seq // tm
    return pl.pallas_call(
        _oproj_kernel,
        grid=(t // tm,),
        in_specs=[pl.BlockSpec((tm, d), lambda i: (i, 0)),
                  pl.BlockSpec((tm, d), lambda i: (i, 0)),
                  pl.BlockSpec((d, d), lambda i: (0, 0)),
                  pl.BlockSpec((1, d), lambda i: (0, 0)),
                  pl.BlockSpec((None, 1, d), lambda i: (i // per_b, 0, 0))],
        out_specs=pl.BlockSpec((tm, d), lambda i: (i, 0)),
        out_shape=jax.ShapeDtypeStruct((t, d), F32),
        compiler_params=_cparams("parallel"),
        name="attn_out_proj",
    )(a, x2, w, g, gate)


def _router_kernel(x_ref, g_ref, sh_ref, sc_ref, wr_ref, h_ref, r_ref):
    h = _norm_mod(x_ref[...], g_ref[...], sh_ref[...], sc_ref[...])
    h_ref[...] = h
    logits = jnp.dot(h, wr_ref[...], preferred_element_type=F32, precision=HIGHEST)
    lane = lax.broadcasted_iota(jnp.int32, logits.shape, 1).astype(F32)
    neg = jnp.float32(-jnp.inf)
    lg = jnp.where(lane < N_EXPERTS, logits, neg)
    m1 = jnp.max(lg, axis=-1, keepdims=True)
    i1 = jnp.min(jnp.where(lg == m1, lane, float(LANES)), axis=-1, keepdims=True)
    lg2 = jnp.where(lane == i1, neg, lg)
    m2 = jnp.max(lg2, axis=-1, keepdims=True)
    i2 = jnp.min(jnp.where(lg2 == m2, lane, float(LANES)), axis=-1, keepdims=True)
    e = jnp.exp(m2 - m1)
    p1 = 1.0 / (1.0 + e)
    p2 = e * p1
    r_ref[...] = jnp.where(lane == 0, i1, jnp.where(lane == 1, i2, jnp.where(lane == 2, p1, jnp.where(lane == 3, p2, 0.0))))


def _router(x2, g, shift, scale, wr_pad, seq, tm=512):
    t, d = x2.shape
    per_b = seq // tm
    mod = pl.BlockSpec((None, 1, d), lambda i: (i // per_b, 0, 0))
    return pl.pallas_call(
        _router_kernel,
        grid=(t // tm,),
        in_specs=[pl.BlockSpec((tm, d), lambda i: (i, 0)),
                  pl.BlockSpec((1, d), lambda i: (0, 0)), mod, mod,
                  pl.BlockSpec((d, LANES), lambda i: (0, 0))],
        out_specs=[pl.BlockSpec((tm, d), lambda i: (i, 0)),
                   pl.BlockSpec((tm, LANES), lambda i: (i, 0))],
        out_shape=[jax.ShapeDtypeStruct((t, d), F32), jax.ShapeDtypeStruct((t, LANES), F32)],
        compiler_params=_cparams("parallel"),
        name="moe_router",
    )(x2, g, shift, scale, wr_pad)


def _row_copy(src_hbm, dst, sem, src_row, dst_row):
    return pltpu.make_async_copy(src_hbm.at[pl.ds(src_row, 1)], dst.at[pl.ds(dst_row, 1)], sem)


def _moe_kernel(te_ref, nu_ref, tok_ref, h_hbm, wg_ref, wu_ref, wd_ref, o_ref, xbuf, xb16, acc, sem):
    i = pl.program_id(0)
    j = pl.program_id(1)
    tm = xbuf.shape[0]
    used = i < nu_ref[0]

    @pl.when(jnp.logical_and(used, j == 0))
    def _():
        def start(r, c):
            _row_copy(h_hbm, xbuf, sem, tok_ref[0, r], r).start()
            return c

        def wait(r, c):
            _row_copy(h_hbm, xbuf, sem, 0, r).wait()
            return c

        lax.fori_loop(0, tm, start, 0)
        lax.fori_loop(0, tm, wait, 0)
        xb16[...] = xbuf[...].astype(BF16)
        acc[...] = jnp.zeros_like(acc)

    @pl.when(used)
    def _():
        x = xb16[...]
        a = _dot(x, wg_ref[...])
        u = _dot(x, wu_ref[...])
        hid = (a * jax.nn.sigmoid(a) * u).astype(BF16)
        acc[...] += _dot(hid, wd_ref[...])

    last = j == pl.num_programs(1) - 1

    @pl.when(jnp.logical_and(used, last))
    def _():
        o_ref[...] = acc[...]

    @pl.when(jnp.logical_and(jnp.logical_not(used), last))
    def _():
        o_ref[...] = jnp.zeros_like(o_ref)


def _moe_experts(tile_expert, n_used, sorted_tok, h, wg, wu, wd, tm=512, tf=512):
    t, d = h.shape
    ne, _, f = wg.shape
    nt = sorted_tok.shape[0]
    nf = f // tf

    def wcol(i, j, te, nu):
        return (te[i], 0, jnp.where(i < nu[0], j, nf - 1))

    def wrow(i, j, te, nu):
        return (te[i], jnp.where(i < nu[0], j, nf - 1), 0)

    grid_spec = pltpu.PrefetchScalarGridSpec(
        num_scalar_prefetch=2,
        grid=(nt, nf),
        in_specs=[pl.BlockSpec((None, 1, tm), lambda i, j, te, nu: (i, 0, 0), memory_space=pltpu.SMEM),
                  pl.BlockSpec(memory_space=pl.ANY),
                  pl.BlockSpec((None, d, tf), wcol),
                  pl.BlockSpec((None, d, tf), wcol),
                  pl.BlockSpec((None, tf, d), wrow)],
        out_specs=pl.BlockSpec((tm, d), lambda i, j, te, nu: (i, 0)),
        scratch_shapes=[pltpu.VMEM((tm, d), F32), pltpu.VMEM((tm, d), BF16), pltpu.VMEM((tm, d), F32),
                        pltpu.SemaphoreType.DMA(())],
    )
    return pl.pallas_call(
        _moe_kernel,
        grid_spec=grid_spec,
        out_shape=jax.ShapeDtypeStruct((nt * tm, d), F32),
        compiler_params=_cparams("arbitrary", "arbitrary"),
        name="moe_experts",
    )(tile_expert, n_used, sorted_tok, h, wg, wu, wd)


def _combine_kernel(d0_ref, d1_ref, y_hbm, r_ref, x_ref, g_ref, gate_ref, o_ref, buf0, buf1, sem):
    tm = x_ref.shape[0]

    def start(r, c):
        _row_copy(y_hbm, buf0, sem.at[0], d0_ref[0, r], r).start()
        _row_copy(y_hbm, buf1, sem.at[1], d1_ref[0, r], r).start()
        return c

    def wait(r, c):
        _row_copy(y_hbm, buf0, sem.at[0], 0, r).wait()
        _row_copy(y_hbm, buf1, sem.at[1], 0, r).wait()
        return c

    lax.fori_loop(0, tm, start, 0)
    lax.fori_loop(0, tm, wait, 0)
    r = r_ref[...]
    f = r[:, 2:3] * buf0[...] + r[:, 3:4] * buf1[...]
    o_ref[...] = x_ref[...] + (1.0 + gate_ref[...]) * (_rms(f) * g_ref[...])


def _moe_combine(dest0, dest1, y, route, x2, g, gate, seq, tm=256):
    t, d = x2.shape
    per_b = seq // tm
    idx = pl.BlockSpec((None, 1, tm), lambda i: (i, 0, 0), memory_space=pltpu.SMEM)
    return pl.pallas_call(
        _combine_kernel,
        grid=(t // tm,),
        in_specs=[idx, idx,
                  pl.BlockSpec(memory_space=pl.ANY),
                  pl.BlockSpec((tm, LANES), lambda i: (i, 0)),
                  pl.BlockSpec((tm, d), lambda i: (i, 0)),
                  pl.BlockSpec((1, d), lambda i: (0, 0)),
                  pl.BlockSpec((None, 1, d), lambda i: (i // per_b, 0, 0))],
        out_specs=pl.BlockSpec((tm, d), lambda i: (i, 0)),
        out_shape=jax.ShapeDtypeStruct((t, d), F32),
        scratch_shapes=[pltpu.VMEM((tm, d), F32), pltpu.VMEM((tm, d), F32), pltpu.SemaphoreType.DMA((2,))],
        compiler_params=_cparams("arbitrary"),
        name="moe_combine",
    )(dest0.reshape(t // tm, 1, tm), dest1.reshape(t // tm, 1, tm), y, route, x2, g, gate)


def _routing_plan(route, tm):
    t = route.shape[0]
    ex = route[:, 0:2].astype(jnp.int32).T.reshape(-1)
    onehot = (ex[:, None] == jnp.arange(N_EXPERTS, dtype=jnp.int32)[None, :]).astype(jnp.int32)
    csum = jnp.cumsum(onehot, axis=0)
    rank = jnp.sum(onehot * csum, axis=1) - 1
    counts = csum[-1]
    tiles = (counts + tm - 1) // tm
    tile_end = jnp.cumsum(tiles)
    offset = (tile_end - tiles) * tm
    dest = offset[ex] + rank
    nt = (2 * t) // tm + N_EXPERTS
    n_used = tile_end[-1]
    tile_ids = jnp.minimum(jnp.arange(nt, dtype=jnp.int32), n_used - 1)
    tile_expert = jnp.sum((tile_ids[:, None] >= tile_end[None, :]).astype(jnp.int32), axis=1)
    token = jnp.arange(2 * t, dtype=jnp.int32) % t
    sorted_tok = jnp.zeros((nt * tm,), jnp.int32).at[dest].set(token)
    return (tile_expert.astype(jnp.int32), n_used.reshape(1).astype(jnp.int32),
            sorted_tok.reshape(nt, 1, tm), dest[:t], dest[t:])


def kernel(x, c, ada_mix_w, ada_mix_b, norm_mix_pre, norm_mix_post, ada_ffn_w, ada_ffn_b, norm_ffn_pre, norm_ffn_post, s5_a_re, s5_a_im, s5_b_re, s5_b_im, s5_c_re, s5_c_im, s5_d, s5_log_step, s5_glu_w, s5_glu_b, kv_ada_w, kv_ada_b, kv_norm, w_k, w_v, w_q, w_o, ffn_w_gate, ffn_w_up, ffn_w_down, moe_w_router, moe_w_gate, moe_w_up, moe_w_down):
    bsz, seq, d = x.shape
    x2 = x.reshape(bsz * seq, d)

    c_pad = jnp.zeros((8, d), F32).at[:bsz].set(c)
    mix = _ada(c_pad, ada_mix_w, ada_mix_b)
    ffn = _ada(c_pad, ada_ffn_w, ada_ffn_b)
    kvm = _ada(c_pad, kv_ada_w[None], kv_ada_b[None])

    def mods(m, layer, n):
        return [m[layer, :bsz, k * d:(k + 1) * d].reshape(bsz, 1, d) for k in range(n)]

    def row(v):
        return v.reshape(1, d)

    shift, scale, gate = mods(mix, 0, 3)
    m1, m2, m3, a1, a2, dsk = _s5_tables(s5_a_re[0], s5_a_im[0], s5_b_re[0], s5_b_im[0],
                                         s5_c_re[0], s5_c_im[0], s5_log_step[0], s5_d[0], bsz)
    ho = _s5_pre(x2, row(norm_mix_pre[0]), shift, scale, seq)
    v = _s5_state(ho, m3, bsz)
    sin = _s5_scan(v, a1, a2, bsz)
    yo = _s5_out(ho, sin, m1, m2, dsk, bsz)
    x2 = _s5_glu(yo, x2, s5_glu_w[0].astype(BF16), row(s5_glu_b[0]), row(norm_mix_post[0]), gate, seq)

    shift, scale, gate = mods(ffn, 0, 3)
    x2 = _ffn(x2, row(norm_ffn_pre[0]), shift, scale, ffn_w_gate[0].astype(BF16), ffn_w_up[0].astype(BF16),
              ffn_w_down[0].astype(BF16), row(norm_ffn_post[0]), gate, seq)

    shift, scale, gate = mods(mix, 1, 3)
    kv_shift, kv_scale = mods(kvm, 0, 2)
    w3 = jnp.stack([w_q[0], w_k, w_v]).astype(BF16)
    qkv = _qkv(x2, row(norm_mix_pre[1]), shift, scale, row(kv_norm), kv_shift, kv_scale, w3, seq)
    att = _attention(qkv, bsz, seq)
    x2 = _oproj(att, x2, w_o[0].astype(BF16), row(norm_mix_post[1]), gate, seq)

    shift, scale, gate = mods(ffn, 1, 3)
    wr_pad = jnp.zeros((d, LANES), F32).at[:, :N_EXPERTS].set(moe_w_router[0])
    h, route = _router(x2, row(norm_ffn_pre[1]), shift, scale, wr_pad, seq)
    tm_moe = 512
    tile_expert, n_used, sorted_tok, dest0, dest1 = _routing_plan(route, tm_moe)
    y = _moe_experts(tile_expert, n_used, sorted_tok, h, moe_w_gate[0].astype(BF16),
                     moe_w_up[0].astype(BF16), moe_w_down[0].astype(BF16), tm=tm_moe)
    x2 = _moe_combine(dest0, dest1, y, route, x2, row(norm_ffn_post[1]), gate, seq)
    return x2.reshape(bsz, seq, d)
```

```python
import functools
import math

import jax
import jax.numpy as jnp
from jax import lax
from jax.experimental import pallas as pl
from jax.experimental.pallas import tpu as pltpu

F32 = jnp.float32
BF16 = jnp.bfloat16

LANES = 128
VMEM_LIMIT = 56 * 1024 * 1024
RMS_EPS = 1e-6

S5_GROUP = 16
S5_STATE = 64
CHUNK = 16
OCT = LANES // S5_GROUP
OCT_STATE = 2 * OCT * S5_STATE
N_HEADS = 16
N_EXPERTS = 8
HIGHEST = lax.Precision.HIGHEST


def _cparams(*semantics):
    return pltpu.CompilerParams(dimension_semantics=semantics, vmem_limit_bytes=VMEM_LIMIT)


def _dot(a, b):
    return jnp.dot(a, b, preferred_element_type=F32)


def _rms(xf):
    ms = jnp.mean(xf * xf, axis=-1, keepdims=True)
    return xf * lax.rsqrt(ms + RMS_EPS)


def _norm_mod(xf, g, shift, scale):
    return _rms(xf) * (g * (1.0 + scale)) + shift


def _ada_kernel(c_ref, w_ref, b_ref, o_ref):
    s = c_ref[...]
    s = s * jax.nn.sigmoid(s)
    o_ref[...] = _dot(s.astype(BF16), w_ref[...].astype(BF16)) + b_ref[...]


def _ada(c_pad, w, b, tn=1024):
    nl, d, n = w.shape
    return pl.pallas_call(
        _ada_kernel,
        grid=(nl, n // tn),
        in_specs=[pl.BlockSpec((8, d), lambda l, j: (0, 0)),
                  pl.BlockSpec((None, d, tn), lambda l, j: (l, 0, j)),
                  pl.BlockSpec((None, 1, tn), lambda l, j: (l, 0, j))],
        out_specs=pl.BlockSpec((None, 8, tn), lambda l, j: (l, 0, j)),
        out_shape=jax.ShapeDtypeStruct((nl, 8, n), F32),
        compiler_params=_cparams("parallel", "parallel"),
        name="ada_mods",
    )(c_pad, w, b.reshape(nl, 1, n))


def _s5_tables(a_re, a_im, b_re, b_im, c_re, c_im, log_step, d_skip, batch):
    g, p = a_re.shape
    h, lc = S5_GROUP, CHUNK
    no = g // OCT
    dt = jnp.exp(log_step)[:, None]
    lr, li = a_re * dt, a_im * dt
    k = jnp.arange(lc + 1, dtype=F32)[:, None, None]
    mag = jnp.exp(k * lr[None])
    pw_r, pw_i = mag * jnp.cos(k * li[None]), mag * jnp.sin(k * li[None])
    nr, ni = pw_r[1] - 1.0, pw_i[1]
    den = a_re * a_re + a_im * a_im
    qr, qi = (nr * a_re + ni * a_im) / den, (ni * a_re - nr * a_im) / den
    bbr = qr[..., None] * b_re - qi[..., None] * b_im
    bbi = qr[..., None] * b_im + qi[..., None] * b_re
    same_group = jnp.eye(OCT, dtype=F32).reshape(1, 1, OCT, 1, 1, OCT, 1)

    wr, wi = pw_r[:lc, :, None, :], pw_i[:lc, :, None, :]
    cwr = c_re[None] * wr - c_im[None] * wi
    cwi = c_re[None] * wi + c_im[None] * wr
    kk = (jnp.einsum('tgop,gpi->tgio', cwr, bbr, precision=HIGHEST)
          - jnp.einsum('tgop,gpi->tgio', cwi, bbi, precision=HIGHEST))
    kk = kk.reshape(lc, no, OCT, h, h).transpose(1, 0, 2, 3, 4)
    kd = (kk[:, :, :, :, None, None, :] * same_group).reshape(no, lc, LANES, LANES)
    kd = jnp.concatenate([jnp.zeros((no, 1, LANES, LANES), F32), kd], axis=1)
    tiles = []
    for delta in range(lc // 2 - 1, -1, -1):
        top = jnp.concatenate([kd[:, 2 * delta + 1], kd[:, 2 * delta + 2]], axis=-1)
        bottom = jnp.concatenate([kd[:, 2 * delta], kd[:, 2 * delta + 1]], axis=-1)
        tiles += [top, bottom]
    m1 = jnp.concatenate(tiles, axis=-2)

    er = c_re[None] * pw_r[1:, :, None, :] - c_im[None] * pw_i[1:, :, None, :]
    ei = c_re[None] * pw_i[1:, :, None, :] + c_im[None] * pw_r[1:, :, None, :]
    e2 = jnp.stack([er, -ei]).reshape(2, lc, no, OCT, h, p).transpose(2, 0, 3, 5, 1, 4)
    m2 = (e2[:, :, :, :, :, None, :] * same_group).reshape(no, OCT_STATE, lc * LANES)

    fr = pw_r[lc - 1::-1][:lc, :, :, None] * bbr[None] - pw_i[lc - 1::-1][:lc, :, :, None] * bbi[None]
    fi = pw_r[lc - 1::-1][:lc, :, :, None] * bbi[None] + pw_i[lc - 1::-1][:lc, :, :, None] * bbr[None]
    f2 = jnp.stack([fr, fi]).reshape(2, lc, no, OCT, p, h).transpose(2, 1, 3, 5, 0, 4)
    m3 = (f2[:, :, :, :, :, None, :] * same_group).reshape(no, lc * LANES, OCT_STATE)

    ar = pw_r[lc].reshape(no, OCT * p)
    ai = pw_i[lc].reshape(no, OCT * p)
    a1 = jnp.tile(jnp.concatenate([ar, ar], axis=-1), (1, batch))[:, None, :]
    a2 = jnp.tile(jnp.concatenate([-ai, ai], axis=-1), (1, batch))[:, None, :]
    dsk = jnp.tile(d_skip.reshape(no, 1, LANES), (1, 1, lc))
    return m1.astype(BF16), m2.astype(BF16), m3.astype(BF16), a1, a2, dsk


def _s5_pre_kernel(x_ref, g_ref, sh_ref, sc_ref, o_ref, slab):
    tm = x_ref.shape[0]
    nt = x_ref.shape[1] // LANES
    rows = tm // CHUNK
    h = _norm_mod(x_ref[...], g_ref[...], sh_ref[...], sc_ref[...])
    for j in range(nt):
        slab[j] = h[:, j * LANES:(j + 1) * LANES]
    for j in range(nt):
        for t in range(CHUNK):
            o_ref[j, :, t * LANES:(t + 1) * LANES] = slab.at[j][pl.ds(t, rows, stride=CHUNK), :].astype(BF16)


def _s5_pre(x2, g, shift, scale, seq, tm=512):
    t, d = x2.shape
    nt = d // LANES
    per_b = seq // tm
    return pl.pallas_call(
        _s5_pre_kernel,
        grid=(t // tm,),
        in_specs=[pl.BlockSpec((tm, d), lambda i: (i, 0)),
                  pl.BlockSpec((1, d), lambda i: (0, 0)),
                  pl.BlockSpec((None, 1, d), lambda i: (i // per_b, 0, 0)),
                  pl.BlockSpec((None, 1, d), lambda i: (i // per_b, 0, 0))],
        out_specs=pl.BlockSpec((nt, tm // CHUNK, CHUNK * LANES), lambda i: (0, i, 0)),
        out_shape=jax.ShapeDtypeStruct((nt, t // CHUNK, CHUNK * LANES), BF16),
        scratch_shapes=[pltpu.VMEM((nt, tm, LANES), F32)],
        compiler_params=_cparams("parallel"),
        name="s5_pre",
    )(x2, g, shift, scale)


def _s5_state_kernel(h_ref, m3_ref, o_ref):
    o_ref[...] = _dot(h_ref[...], m3_ref[...])


def _s5_state(ho, m3, batch):
    no, r, kc = ho.shape
    c = r // batch
    return pl.pallas_call(
        _s5_state_kernel,
        grid=(no, batch),
        in_specs=[pl.BlockSpec((None, c, kc), lambda j, b: (j, b, 0)),
                  pl.BlockSpec((None, kc, OCT_STATE), lambda j, b: (j, 0, 0))],
        out_specs=pl.BlockSpec((None, c, OCT_STATE), lambda j, b: (j, 0, b)),
        out_shape=jax.ShapeDtypeStruct((no, c, batch * OCT_STATE), F32),
        compiler_params=_cparams("parallel", "parallel"),
        name="s5_state",
    )(ho, m3)


def _s5_scan_kernel(v_ref, a1_ref, a2_ref, o_ref, *, batch, unroll):
    a1 = a1_ref[...]
    a2 = a2_ref[...]
    half = OCT_STATE // 2

    def swap(s):
        parts = []
        for b in range(batch):
            parts.append(s[:, b * OCT_STATE + half:(b + 1) * OCT_STATE])
            parts.append(s[:, b * OCT_STATE:b * OCT_STATE + half])
        return jnp.concatenate(parts, axis=-1)

    def body(i, s):
        base = pl.multiple_of(i * unroll, unroll)
        v = v_ref[pl.ds(base, unroll), :]
        rows = []
        for k in range(unroll):
            rows.append(s)
            s = a1 * s + a2 * swap(s) + v[k:k + 1, :]
        o_ref[pl.ds(base, unroll), :] = jnp.concatenate(rows, axis=0).astype(o_ref.dtype)
        return s

    lax.fori_loop(0, v_ref.shape[0] // unroll, body, jnp.zeros((1, v_ref.shape[1]), F32))


def _s5_scan(v, a1, a2, batch, unroll=16):
    no, c, w = v.shape
    return pl.pallas_call(
        functools.partial(_s5_scan_kernel, batch=batch, unroll=unroll),
        grid=(no,),
        in_specs=[pl.BlockSpec((None, c, w), lambda j: (j, 0, 0)),
                  pl.BlockSpec((None, 1, w), lambda j: (j, 0, 0)),
                  pl.BlockSpec((None, 1, w), lambda j: (j, 0, 0))],
        out_specs=pl.BlockSpec((None, c, w), lambda j: (j, 0, 0)),
        out_shape=jax.ShapeDtypeStruct((no, c, w), BF16),
        compiler_params=_cparams("parallel"),
        name="s5_scan",
    )(v, a1, a2)


def _s5_out_kernel(h_ref, s_ref, m1_ref, m2_ref, d_ref, o_ref):
    tw = m1_ref.shape[1]
    n_tiles = h_ref.shape[1] // tw
    for tile in range(n_tiles):
        cols = slice(tile * tw, (tile + 1) * tw)
        y = (_dot(h_ref[:, :(tile + 1) * tw], m1_ref[(n_tiles - 1 - tile) * tw:, :])
             + _dot(s_ref[...], m2_ref[:, cols])
             + d_ref[:, cols] * h_ref[:, cols].astype(F32))
        o_ref[:, cols] = jax.nn.gelu(y).astype(BF16)


def _s5_out(ho, sin, m1, m2, dsk, batch):
    no, r, kc = ho.shape
    c = r // batch
    return pl.pallas_call(
        _s5_out_kernel,
        grid=(no, batch),
        in_specs=[pl.BlockSpec((None, c, kc), lambda j, b: (j, b, 0)),
                  pl.BlockSpec((None, c, OCT_STATE), lambda j, b: (j, 0, b)),
                  pl.BlockSpec((None, kc, m1.shape[2]), lambda j, b: (j, 0, 0)),
                  pl.BlockSpec((None, OCT_STATE, kc), lambda j, b: (j, 0, 0)),
                  pl.BlockSpec((None, 1, kc), lambda j, b: (j, 0, 0))],
        out_specs=pl.BlockSpec((None, c, kc), lambda j, b: (j, b, 0)),
        out_shape=jax.ShapeDtypeStruct((no, r, kc), BF16),
        compiler_params=_cparams("parallel", "parallel"),
        name="s5_out",
    )(ho, sin, m1, m2, dsk)


def _s5_glu_kernel(y_ref, x_ref, w_ref, b_ref, g_ref, gate_ref, o_ref, ybuf, slab):
    nt, rows, _ = y_ref.shape
    for t in range(CHUNK):
        for j in range(nt):
            ybuf[t * rows:(t + 1) * rows, j * LANES:(j + 1) * LANES] = y_ref[j, :, t * LANES:(t + 1) * LANES]
    y = ybuf[...]
    z = _dot(y, w_ref[...]) + b_ref[...]
    m = y.astype(F32) * jax.nn.sigmoid(z)
    res = (1.0 + gate_ref[...]) * (_rms(m) * g_ref[...])
    for j in range(nt):
        for t in range(CHUNK):
            slab.at[j][pl.ds(t, rows, stride=CHUNK), :] = res[t * rows:(t + 1) * rows, j * LANES:(j + 1) * LANES]
    for j in range(nt):
        o_ref[:, j * LANES:(j + 1) * LANES] = x_ref[:, j * LANES:(j + 1) * LANES] + slab[j]


def _s5_glu(yo, x2, w, b, g, gate, seq, tm=512):
    t, d = x2.shape
    nt = d // LANES
    per_b = seq // tm
    return pl.pallas_call(
        _s5_glu_kernel,
        grid=(t // tm,),
        in_specs=[pl.BlockSpec((nt, tm // CHUNK, CHUNK * LANES), lambda i: (0, i, 0)),
                  pl.BlockSpec((tm, d), lambda i: (i, 0)),
                  pl.BlockSpec((d, d), lambda i: (0, 0)),
                  pl.BlockSpec((1, d), lambda i: (0, 0)),
                  pl.BlockSpec((1, d), lambda i: (0, 0)),
                  pl.BlockSpec((None, 1, d), lambda i: (i // per_b, 0, 0))],
        out_specs=pl.BlockSpec((tm, d), lambda i: (i, 0)),
        out_shape=jax.ShapeDtypeStruct((t, d), F32),
        scratch_shapes=[pltpu.VMEM((tm, d), BF16), pltpu.VMEM((nt, tm, LANES), F32)],
        compiler_params=_cparams("parallel"),
        name="s5_glu",
    )(yo, x2, w, b, g, gate)


def _interleave_gate_up(wg, wu):
    *lead, d, f = wg.shape
    g = wg.astype(BF16).reshape(*lead, d, f // LANES, 1, LANES)
    u = wu.astype(BF16).reshape(*lead, d, f // LANES, 1, LANES)
    return jnp.concatenate([g, u], axis=-2).reshape(*lead, d, 2 * f)


def _swiglu_hidden(h, wgu_ref, hid_scr):
    for s in range(hid_scr.shape[1] // LANES):
        r = _dot(h, wgu_ref[:, 2 * s * LANES:(2 * s + 2) * LANES])
        a, u = r[:, :LANES], r[:, LANES:]
        hid_scr[:, s * LANES:(s + 1) * LANES] = (a * jax.nn.sigmoid(a) * u).astype(BF16)


def _ffn_kernel(x_ref, g_ref, sh_ref, sc_ref, wgu_ref, wd_ref, gp_ref, gate_ref, o_ref, h_scr, hid_scr, acc):
    j = pl.program_id(1)

    @pl.when(j == 0)
    def _():
        h_scr[...] = _norm_mod(x_ref[...], g_ref[...], sh_ref[...], sc_ref[...]).astype(BF16)
        acc[...] = jnp.zeros_like(acc)

    _swiglu_hidden(h_scr[...], wgu_ref, hid_scr)
    acc[...] += _dot(hid_scr[...], wd_ref[...])

    @pl.when(j == pl.num_programs(1) - 1)
    def _():
        o_ref[...] = x_ref[...] + (1.0 + gate_ref[...]) * (_rms(acc[...]) * gp_ref[...])


def _ffn(x2, g, shift, scale, wgu, wd, gp, gate, seq, tm=512, tf=512):
    t, d = x2.shape
    f = wd.shape[0]
    per_b = seq // tm
    mod = pl.BlockSpec((None, 1, d), lambda i, j: (i // per_b, 0, 0))
    row = pl.BlockSpec((1, d), lambda i, j: (0, 0))
    return pl.pallas_call(
        _ffn_kernel,
        grid=(t // tm, f // tf),
        in_specs=[pl.BlockSpec((tm, d), lambda i, j: (i, 0)), row, mod, mod,
                  pl.BlockSpec((d, 2 * tf), lambda i, j: (0, j)),
                  pl.BlockSpec((tf, d), lambda i, j: (j, 0)),
                  row, mod],
        out_specs=pl.BlockSpec((tm, d), lambda i, j: (i, 0)),
        out_shape=jax.ShapeDtypeStruct((t, d), F32),
        scratch_shapes=[pltpu.VMEM((tm, d), BF16), pltpu.VMEM((tm, tf), BF16), pltpu.VMEM((tm, d), F32)],
        compiler_params=_cparams("parallel", "arbitrary"),
        name="dense_ffn",
    )(x2, g, shift, scale, wgu, wd, gp, gate)


def _qkv_kernel(x_ref, gq_ref, shq_ref, scq_ref, gkv_ref, shkv_ref, sckv_ref, w_ref, o_ref, hq, hkv, *, qscale):
    n = pl.program_id(1)

    @pl.when(n == 0)
    def _():
        xf = x_ref[...]
        xn = _rms(xf)
        hq[...] = (xn * (gq_ref[...] * (1.0 + scq_ref[...])) + shq_ref[...]).astype(BF16)
        hkv[...] = (xn * (gkv_ref[...] * (1.0 + sckv_ref[...])) + shkv_ref[...]).astype(BF16)
        o_ref[...] = (_dot(hq[...], w_ref[...]) * qscale).astype(BF16)

    @pl.when(n > 0)
    def _():
        o_ref[...] = _dot(hkv[...], w_ref[...]).astype(BF16)


def _qkv(x2, gq, shq, scq, gkv, shkv, sckv, w3, seq, tm=512):
    t, d = x2.shape
    per_b = seq // tm
    mod = pl.BlockSpec((None, 1, d), lambda i, n: (i // per_b, 0, 0))
    row = pl.BlockSpec((1, d), lambda i, n: (0, 0))
    qscale = 1.0 / math.sqrt(d // N_HEADS)
    return pl.pallas_call(
        functools.partial(_qkv_kernel, qscale=qscale),
        grid=(t // tm, 3),
        in_specs=[pl.BlockSpec((tm, d), lambda i, n: (i, 0)), row, mod, mod, row, mod, mod,
                  pl.BlockSpec((None, d, d), lambda i, n: (n, 0, 0))],
        out_specs=pl.BlockSpec((None, tm, d), lambda i, n: (n, i, 0)),
        out_shape=jax.ShapeDtypeStruct((3, t, d), BF16),
        scratch_shapes=[pltpu.VMEM((tm, d), BF16), pltpu.VMEM((tm, d), BF16)],
        compiler_params=_cparams("parallel", "arbitrary"),
        name="qkv_proj",
    )(x2, gq, shq, scq, gkv, shkv, sckv, w3)


EXP_ZERO_BELOW = -104.0


def _attn_kernel(q_ref, k_ref, v_ref, o_ref, *, tb, nh, hd):
    qi = pl.program_id(2)
    row = lax.broadcasted_iota(jnp.int32, (tb, tb), 0)
    col = lax.broadcasted_iota(jnp.int32, (tb, tb), 1)
    later = jnp.where(row > col, 1.0, 0.0).astype(BF16)
    valid = col < row

    def block(kb, accs, runs, diagonal):
        start = pl.multiple_of(kb * tb, tb)
        new_accs, new_runs = [], []
        for g in range(nh):
            lanes = slice(g * hd, (g + 1) * hd)
            k = k_ref[pl.ds(start, tb), lanes]
            v = v_ref[pl.ds(start, tb), lanes]
            z = lax.dot_general(q_ref[:, lanes], k, (((1,), (1,)), ((), ())), preferred_element_type=F32)
            sp = jnp.maximum(z, 0.0) + jnp.log(1.0 + jnp.exp(-jnp.abs(z)))
            l1m = -sp
            if diagonal:
                l1m = jnp.where(valid, l1m, 0.0)
            hi = l1m.astype(BF16)
            lo = (l1m - hi.astype(F32)).astype(BF16)
            suffix = _dot(hi, later) + _dot(lo, later)
            w = jnp.exp((z - sp) + suffix + runs[g])
            if diagonal:
                w = jnp.where(valid, w, 0.0)
            new_accs.append(accs[g] + _dot(w.astype(BF16), v))
            new_runs.append(runs[g] + suffix[:, 0:1] + l1m[:, 0:1])
        return tuple(new_accs), tuple(new_runs)

    accs = tuple(jnp.zeros((tb, hd), F32) for _ in range(nh))
    runs = tuple(jnp.zeros((tb, 1), F32) for _ in range(nh))
    accs, runs = block(qi, accs, runs, True)

    def cond(c):
        kb, _, runs = c
        top = runs[0]
        for g in range(1, nh):
            top = jnp.maximum(top, runs[g])
        return jnp.logical_and(kb >= 0, jnp.max(top) >= EXP_ZERO_BELOW)

    def body(c):
        kb, accs, runs = c
        accs, runs = block(kb, accs, runs, False)
        return kb - 1, accs, runs

    _, accs, _ = lax.while_loop(cond, body, (qi - 1, accs, runs))
    for g in range(nh):
        o_ref[:, g * hd:(g + 1) * hd] = accs[g].astype(o_ref.dtype)


def _attention(qkv, batch, seq, tb=128, nh=4):
    _, t, d = qkv.shape
    hd = d // N_HEADS
    nq = seq // tb
    return pl.pallas_call(
        functools.partial(_attn_kernel, tb=tb, nh=nh, hd=hd),
        grid=(batch, N_HEADS // nh, nq),
        in_specs=[pl.BlockSpec((None, tb, nh * hd), lambda b, h, i: (0, b * nq + i, h)),
                  pl.BlockSpec((None, seq, nh * hd), lambda b, h, i: (1, b, h)),
                  pl.BlockSpec((None, seq, nh * hd), lambda b, h, i: (2, b, h))],
        out_specs=pl.BlockSpec((tb, nh * hd), lambda b, h, i: (b * nq + i, h)),
        out_shape=jax.ShapeDtypeStruct((t, d), BF16),
        compiler_params=_cparams("parallel", "parallel", "parallel"),
        name="stickbreak_attn",
    )(qkv, qkv, qkv)


def _oproj_kernel(a_ref, x_ref, w_ref, g_ref, gate_ref, o_ref):
    m = _dot(a_ref[...], w_ref[...])
    o_ref[...] = x_ref[...] + (1.0 + gate_ref[...]) * (_rms(m) * g_ref[...])


def _oproj(a, x2, w, g, gate, seq, tm=512):
    t, d = x2.shape
    per_b = seq // tm
    return pl.pallas_call(
        _oproj_kernel,
        grid=(t // tm,),
        in_specs=[pl.BlockSpec((tm, d), lambda i: (i, 0)),
                  pl.BlockSpec((tm, d), lambda i: (i, 0)),
                  pl.BlockSpec((d, d), lambda i: (0, 0)),
                  pl.BlockSpec((1, d), lambda i: (0, 0)),
                  pl.BlockSpec((None, 1, d), lambda i: (i // per_b, 0, 0))],
        out_specs=pl.BlockSpec((tm, d), lambda i: (i, 0)),
        out_shape=jax.ShapeDtypeStruct((t, d), F32),
        compiler_params=_cparams("parallel"),
        name="attn_out_proj",
    )(a, x2, w, g, gate)


def _router_kernel(x_ref, g_ref, sh_ref, sc_ref, wr_ref, h_ref, r_ref):
    h = _norm_mod(x_ref[...], g_ref[...], sh_ref[...], sc_ref[...])
    h_ref[...] = h
    logits = jnp.dot(h, wr_ref[...], preferred_element_type=F32, precision=HIGHEST)
    lane = lax.broadcasted_iota(jnp.int32, logits.shape, 1).astype(F32)
    neg = jnp.float32(-jnp.inf)
    lg = jnp.where(lane < N_EXPERTS, logits, neg)
    m1 = jnp.max(lg, axis=-1, keepdims=True)
    i1 = jnp.min(jnp.where(lg == m1, lane, float(LANES)), axis=-1, keepdims=True)
    lg2 = jnp.where(lane == i1, neg, lg)
    m2 = jnp.max(lg2, axis=-1, keepdims=True)
    i2 = jnp.min(jnp.where(lg2 == m2, lane, float(LANES)), axis=-1, keepdims=True)
    e = jnp.exp(m2 - m1)
    p1 = 1.0 / (1.0 + e)
    p2 = e * p1
    r_ref[...] = jnp.where(lane == 0, i1, jnp.where(lane == 1, i2, jnp.where(lane == 2, p1, jnp.where(lane == 3, p2, 0.0))))


def _router(x2, g, shift, scale, wr_pad, seq, tm=512):
    t, d = x2.shape
    per_b = seq // tm
    mod = pl.BlockSpec((None, 1, d), lambda i: (i // per_b, 0, 0))
    return pl.pallas_call(
        _router_kernel,
        grid=(t // tm,),
        in_specs=[pl.BlockSpec((tm, d), lambda i: (i, 0)),
                  pl.BlockSpec((1, d), lambda i: (0, 0)), mod, mod,
                  pl.BlockSpec((d, LANES), lambda i: (0, 0))],
        out_specs=[pl.BlockSpec((tm, d), lambda i: (i, 0)),
                   pl.BlockSpec((tm, LANES), lambda i: (i, 0))],
        out_shape=[jax.ShapeDtypeStruct((t, d), F32), jax.ShapeDtypeStruct((t, LANES), F32)],
        compiler_params=_cparams("parallel"),
        name="moe_router",
    )(x2, g, shift, scale, wr_pad)


def _row_copy(src_hbm, dst, sem, src_row, dst_row):
    return pltpu.make_async_copy(src_hbm.at[pl.ds(src_row, 1)], dst.at[pl.ds(dst_row, 1)], sem)


def _moe_kernel(te_ref, nu_ref, tok_ref, nxt_ref, h_hbm, wgu_ref, wd_ref, o_ref, xbuf, xb16, hid_scr, acc, sem):
    i = pl.program_id(0)
    j = pl.program_id(1)
    tm = xb16.shape[0]
    used = i < nu_ref[0]
    slot = lax.rem(i, 2)

    def gather(idx_ref, s):
        def start(r, c):
            _row_copy(h_hbm, xbuf.at[s], sem.at[s], idx_ref[0, r], r).start()
            return c
        lax.fori_loop(0, tm, start, 0, unroll=8)

    @pl.when(jnp.logical_and(i == 0, j == 0))
    def _():
        gather(tok_ref, 0)

    @pl.when(jnp.logical_and(i + 1 < nu_ref[0], j == 0))
    def _():
        gather(nxt_ref, 1 - slot)

    @pl.when(jnp.logical_and(used, j == 0))
    def _():
        def wait(r, c):
            _row_copy(h_hbm, xbuf.at[slot], sem.at[slot], 0, r).wait()
            return c
        lax.fori_loop(0, tm, wait, 0, unroll=8)
        xb16[...] = xbuf[slot].astype(BF16)
        acc[...] = jnp.zeros_like(acc)

    @pl.when(used)
    def _():
        _swiglu_hidden(xb16[...], wgu_ref, hid_scr)
        acc[...] += _dot(hid_scr[...], wd_ref[...])

    last = j == pl.num_programs(1) - 1

    @pl.when(jnp.logical_and(used, last))
    def _():
        o_ref[...] = acc[...]

    @pl.when(jnp.logical_and(jnp.logical_not(used), last))
    def _():
        o_ref[...] = jnp.zeros_like(o_ref)


def _moe_experts(tile_expert, n_used, sorted_tok, h, wgu, wd, tm=512, tf=512):
    t, d = h.shape
    f = wd.shape[1]
    nt = sorted_tok.shape[0]
    nf = f // tf

    def wcol(i, j, te, nu):
        return (te[i], 0, jnp.where(i < nu[0], j, nf - 1))

    def wrow(i, j, te, nu):
        return (te[i], jnp.where(i < nu[0], j, nf - 1), 0)

    grid_spec = pltpu.PrefetchScalarGridSpec(
        num_scalar_prefetch=2,
        grid=(nt, nf),
        in_specs=[pl.BlockSpec((None, 1, tm), lambda i, j, te, nu: (i, 0, 0), memory_space=pltpu.SMEM),
                  pl.BlockSpec((None, 1, tm), lambda i, j, te, nu: (jnp.minimum(i + 1, nt - 1), 0, 0),
                               memory_space=pltpu.SMEM),
                  pl.BlockSpec(memory_space=pl.ANY),
                  pl.BlockSpec((None, d, 2 * tf), wcol),
                  pl.BlockSpec((None, tf, d), wrow)],
        out_specs=pl.BlockSpec((tm, d), lambda i, j, te, nu: (i, 0)),
        scratch_shapes=[pltpu.VMEM((2, tm, d), F32), pltpu.VMEM((tm, d), BF16), pltpu.VMEM((tm, tf), BF16),
                        pltpu.VMEM((tm, d), F32), pltpu.SemaphoreType.DMA((2,))],
    )
    return pl.pallas_call(
        _moe_kernel,
        grid_spec=grid_spec,
        out_shape=jax.ShapeDtypeStruct((nt * tm, d), F32),
        compiler_params=_cparams("arbitrary", "arbitrary"),
        name="moe_experts",
    )(tile_expert, n_used, sorted_tok, sorted_tok, h, wgu, wd)


def _combine_kernel(d0_ref, d1_ref, n0_ref, n1_ref, y_hbm, r_ref, x_ref, g_ref, gate_ref, o_ref, buf, sem):
    i = pl.program_id(0)
    tm = x_ref.shape[0]
    slot = lax.rem(i, 2)

    def gather(i0_ref, i1_ref, s):
        def start(r, c):
            _row_copy(y_hbm, buf.at[s, 0], sem.at[s], i0_ref[0, r], r).start()
            _row_copy(y_hbm, buf.at[s, 1], sem.at[s], i1_ref[0, r], r).start()
            return c
        lax.fori_loop(0, tm, start, 0, unroll=8)

    @pl.when(i == 0)
    def _():
        gather(d0_ref, d1_ref, 0)

    @pl.when(i + 1 < pl.num_programs(0))
    def _():
        gather(n0_ref, n1_ref, 1 - slot)

    def wait(r, c):
        _row_copy(y_hbm, buf.at[slot, 0], sem.at[slot], 0, r).wait()
        _row_copy(y_hbm, buf.at[slot, 1], sem.at[slot], 0, r).wait()
        return c

    lax.fori_loop(0, tm, wait, 0, unroll=8)
    r = r_ref[...]
    f = r[:, 2:3] * buf[slot, 0] + r[:, 3:4] * buf[slot, 1]
    o_ref[...] = x_ref[...] + (1.0 + gate_ref[...]) * (_rms(f) * g_ref[...])


def _moe_combine(dest0, dest1, y, route, x2, g, gate, seq, tm=256):
    t, d = x2.shape
    per_b = seq // tm
    nt = t // tm
    idx = pl.BlockSpec((None, 1, tm), lambda i: (i, 0, 0), memory_space=pltpu.SMEM)
    nxt = pl.BlockSpec((None, 1, tm), lambda i: (jnp.minimum(i + 1, nt - 1), 0, 0), memory_space=pltpu.SMEM)
    d0 = dest0.reshape(nt, 1, tm)
    d1 = dest1.reshape(nt, 1, tm)
    return pl.pallas_call(
        _combine_kernel,
        grid=(nt,),
        in_specs=[idx, idx, nxt, nxt,
                  pl.BlockSpec(memory_space=pl.ANY),
                  pl.BlockSpec((tm, LANES), lambda i: (i, 0)),
                  pl.BlockSpec((tm, d), lambda i: (i, 0)),
                  pl.BlockSpec((1, d), lambda i: (0, 0)),
                  pl.BlockSpec((None, 1, d), lambda i: (i // per_b, 0, 0))],
        out_specs=pl.BlockSpec((tm, d), lambda i: (i, 0)),
        out_shape=jax.ShapeDtypeStruct((t, d), F32),
        scratch_shapes=[pltpu.VMEM((2, 2, tm, d), F32), pltpu.SemaphoreType.DMA((2,))],
        compiler_params=_cparams("arbitrary"),
        name="moe_combine",
    )(d0, d1, d0, d1, y, route, x2, g, gate)


def _routing_plan(route, tm):
    t = route.shape[0]
    ex = route[:, 0:2].astype(jnp.int32).T.reshape(-1)
    onehot = (ex[:, None] == jnp.arange(N_EXPERTS, dtype=jnp.int32)[None, :]).astype(jnp.int32)
    csum = jnp.cumsum(onehot, axis=0)
    rank = jnp.sum(onehot * csum, axis=1) - 1
    counts = csum[-1]
    tiles = (counts + tm - 1) // tm
    tile_end = jnp.cumsum(tiles)
    offset = (tile_end - tiles) * tm
    dest = offset[ex] + rank
    nt = (2 * t) // tm + N_EXPERTS
    n_used = tile_end[-1]
    tile_ids = jnp.minimum(jnp.arange(nt, dtype=jnp.int32), n_used - 1)
    tile_expert = jnp.sum((tile_ids[:, None] >= tile_end[None, :]).astype(jnp.int32), axis=1)
    token = jnp.arange(2 * t, dtype=jnp.int32) % t
    sorted_tok = jnp.zeros((nt * tm,), jnp.int32).at[dest].set(token)
    return (tile_expert.astype(jnp.int32), n_used.reshape(1).astype(jnp.int32),
            sorted_tok.reshape(nt, 1, tm), dest[:t], dest[t:])


def kernel(x, c, ada_mix_w, ada_mix_b, norm_mix_pre, norm_mix_post, ada_ffn_w, ada_ffn_b, norm_ffn_pre, norm_ffn_post, s5_a_re, s5_a_im, s5_b_re, s5_b_im, s5_c_re, s5_c_im, s5_d, s5_log_step, s5_glu_w, s5_glu_b, kv_ada_w, kv_ada_b, kv_norm, w_k, w_v, w_q, w_o, ffn_w_gate, ffn_w_up, ffn_w_down, moe_w_router, moe_w_gate, moe_w_up, moe_w_down):
    bsz, seq, d = x.shape
    x2 = x.reshape(bsz * seq, d)

    c_pad = jnp.zeros((8, d), F32).at[:bsz].set(c)
    mix = _ada(c_pad, ada_mix_w, ada_mix_b)
    ffn = _ada(c_pad, ada_ffn_w, ada_ffn_b)
    kvm = _ada(c_pad, kv_ada_w[None], kv_ada_b[None])

    def mods(m, layer, n):
        return [m[layer, :bsz, k * d:(k + 1) * d].reshape(bsz, 1, d) for k in range(n)]

    def row(v):
        return v.reshape(1, d)

    shift, scale, gate = mods(mix, 0, 3)
    m1, m2, m3, a1, a2, dsk = _s5_tables(s5_a_re[0], s5_a_im[0], s5_b_re[0], s5_b_im[0],
                                         s5_c_re[0], s5_c_im[0], s5_log_step[0], s5_d[0], bsz)
    ho = _s5_pre(x2, row(norm_mix_pre[0]), shift, scale, seq)
    v = _s5_state(ho, m3, bsz)
    sin = _s5_scan(v, a1, a2, bsz)
    yo = _s5_out(ho, sin, m1, m2, dsk, bsz)
    x2 = _s5_glu(yo, x2, s5_glu_w[0].astype(BF16), row(s5_glu_b[0]), row(norm_mix_post[0]), gate, seq)

    shift, scale, gate = mods(ffn, 0, 3)
    x2 = _ffn(x2, row(norm_ffn_pre[0]), shift, scale, _interleave_gate_up(ffn_w_gate[0], ffn_w_up[0]),
              ffn_w_down[0].astype(BF16), row(norm_ffn_post[0]), gate, seq)

    shift, scale, gate = mods(mix, 1, 3)
    kv_shift, kv_scale = mods(kvm, 0, 2)
    w3 = jnp.stack([w_q[0], w_k, w_v]).astype(BF16)
    qkv = _qkv(x2, row(norm_mix_pre[1]), shift, scale, row(kv_norm), kv_shift, kv_scale, w3, seq)
    att = _attention(qkv, bsz, seq)
    x2 = _oproj(att, x2, w_o[0].astype(BF16), row(norm_mix_post[1]), gate, seq)

    shift, scale, gate = mods(ffn, 1, 3)
    wr_pad = jnp.zeros((d, LANES), F32).at[:, :N_EXPERTS].set(moe_w_router[0])
    h, route = _router(x2, row(norm_ffn_pre[1]), shift, scale, wr_pad, seq)
    tm_moe = 512
    tile_expert, n_used, sorted_tok, dest0, dest1 = _routing_plan(route, tm_moe)
    y = _moe_experts(tile_expert, n_used, sorted_tok, h, _interleave_gate_up(moe_w_gate[0], moe_w_up[0]),
                     moe_w_down[0].astype(BF16), tm=tm_moe)
    x2 = _moe_combine(dest0, dest1, y, route, x2, row(norm_ffn_post[1]), gate, seq)
    return x2.reshape(bsz, seq, d)
```

```python
import functools
import math

import jax
import jax.numpy as jnp
from jax import lax
from jax.experimental import pallas as pl
from jax.experimental.pallas import tpu as pltpu

F32 = jnp.float32
BF16 = jnp.bfloat16

LANES = 128
VMEM_LIMIT = 56 * 1024 * 1024
RMS_EPS = 1e-6

S5_GROUP = 16
S5_STATE = 64
CHUNK = 16
OCT = LANES // S5_GROUP
OCT_STATE = 2 * OCT * S5_STATE
N_HEADS = 16
N_EXPERTS = 8
HIGHEST = lax.Precision.HIGHEST


def _cparams(*semantics):
    return pltpu.CompilerParams(dimension_semantics=semantics, vmem_limit_bytes=VMEM_LIMIT)


def _dot(a, b):
    return jnp.dot(a, b, preferred_element_type=F32)


def _rms(xf):
    ms = jnp.mean(xf * xf, axis=-1, keepdims=True)
    return xf * lax.rsqrt(ms + RMS_EPS)


def _norm_mod(xf, g, shift, scale):
    return _rms(xf) * (g * (1.0 + scale)) + shift


def _ada_kernel(c_ref, w_ref, b_ref, o_ref):
    s = c_ref[...]
    s = s * jax.nn.sigmoid(s)
    o_ref[...] = _dot(s.astype(BF16), w_ref[...].astype(BF16)) + b_ref[...]


def _ada(c_pad, w, b, tn=1024):
    nl, d, n = w.shape
    return pl.pallas_call(
        _ada_kernel,
        grid=(nl, n // tn),
        in_specs=[pl.BlockSpec((8, d), lambda l, j: (0, 0)),
                  pl.BlockSpec((None, d, tn), lambda l, j: (l, 0, j)),
                  pl.BlockSpec((None, 1, tn), lambda l, j: (l, 0, j))],
        out_specs=pl.BlockSpec((None, 8, tn), lambda l, j: (l, 0, j)),
        out_shape=jax.ShapeDtypeStruct((nl, 8, n), F32),
        compiler_params=_cparams("parallel", "parallel"),
        name="ada_mods",
    )(c_pad, w, b.reshape(nl, 1, n))


def _s5_tables(a_re, a_im, b_re, b_im, c_re, c_im, log_step, d_skip, batch):
    g, p = a_re.shape
    h, lc = S5_GROUP, CHUNK
    no = g // OCT
    dt = jnp.exp(log_step)[:, None]
    lr, li = a_re * dt, a_im * dt
    k = jnp.arange(lc + 1, dtype=F32)[:, None, None]
    mag = jnp.exp(k * lr[None])
    pw_r, pw_i = mag * jnp.cos(k * li[None]), mag * jnp.sin(k * li[None])
    nr, ni = pw_r[1] - 1.0, pw_i[1]
    den = a_re * a_re + a_im * a_im
    qr, qi = (nr * a_re + ni * a_im) / den, (ni * a_re - nr * a_im) / den
    bbr = qr[..., None] * b_re - qi[..., None] * b_im
    bbi = qr[..., None] * b_im + qi[..., None] * b_re
    same_group = jnp.eye(OCT, dtype=F32).reshape(1, 1, OCT, 1, 1, OCT, 1)

    wr, wi = pw_r[:lc, :, None, :], pw_i[:lc, :, None, :]
    cwr = c_re[None] * wr - c_im[None] * wi
    cwi = c_re[None] * wi + c_im[None] * wr
    kk = (jnp.einsum('tgop,gpi->tgio', cwr, bbr, precision=HIGHEST)
          - jnp.einsum('tgop,gpi->tgio', cwi, bbi, precision=HIGHEST))
    kk = kk.reshape(lc, no, OCT, h, h).transpose(1, 0, 2, 3, 4)
    kd = (kk[:, :, :, :, None, None, :] * same_group).reshape(no, lc, LANES, LANES)
    kd = jnp.concatenate([jnp.zeros((no, 1, LANES, LANES), F32), kd], axis=1)
    tiles = []
    for delta in range(lc // 2 - 1, -1, -1):
        top = jnp.concatenate([kd[:, 2 * delta + 1], kd[:, 2 * delta + 2]], axis=-1)
        bottom = jnp.concatenate([kd[:, 2 * delta], kd[:, 2 * delta + 1]], axis=-1)
        tiles += [top, bottom]
    m1 = jnp.concatenate(tiles, axis=-2)

    er = c_re[None] * pw_r[1:, :, None, :] - c_im[None] * pw_i[1:, :, None, :]
    ei = c_re[None] * pw_i[1:, :, None, :] + c_im[None] * pw_r[1:, :, None, :]
    e2 = jnp.stack([er, -ei]).reshape(2, lc, no, OCT, h, p).transpose(2, 0, 3, 5, 1, 4)
    m2 = (e2.astype(BF16)[:, :, :, :, :, None, :] * same_group.astype(BF16)).reshape(no, OCT_STATE, lc * LANES)

    fr = pw_r[lc - 1::-1][:lc, :, :, None] * bbr[None] - pw_i[lc - 1::-1][:lc, :, :, None] * bbi[None]
    fi = pw_r[lc - 1::-1][:lc, :, :, None] * bbi[None] + pw_i[lc - 1::-1][:lc, :, :, None] * bbr[None]
    f2 = jnp.stack([fr, fi]).reshape(2, lc, no, OCT, p, h).transpose(2, 1, 3, 5, 0, 4)
    m3 = (f2.astype(BF16)[:, :, :, :, :, None, :] * same_group.astype(BF16)).reshape(no, lc * LANES, OCT_STATE)

    ar = pw_r[lc].reshape(no, OCT * p)
    ai = pw_i[lc].reshape(no, OCT * p)
    a1 = jnp.tile(jnp.concatenate([ar, ar], axis=-1), (1, batch))[:, None, :]
    a2 = jnp.tile(jnp.concatenate([-ai, ai], axis=-1), (1, batch))[:, None, :]
    dsk = jnp.tile(d_skip.reshape(no, 1, LANES), (1, 1, lc))
    return m1.astype(BF16), m2.astype(BF16), m3.astype(BF16), a1, a2, dsk


def _s5_pre_kernel(x_ref, g_ref, sh_ref, sc_ref, o_ref, slab):
    tm = x_ref.shape[0]
    nt = x_ref.shape[1] // LANES
    rows = tm // CHUNK
    h = _norm_mod(x_ref[...], g_ref[...], sh_ref[...], sc_ref[...])
    for j in range(nt):
        slab[j] = h[:, j * LANES:(j + 1) * LANES]
    for j in range(nt):
        for t in range(CHUNK):
            o_ref[j, :, t * LANES:(t + 1) * LANES] = slab.at[j][pl.ds(t, rows, stride=CHUNK), :].astype(BF16)


def _s5_pre(x2, g, shift, scale, seq, tm=512):
    t, d = x2.shape
    nt = d // LANES
    per_b = seq // tm
    return pl.pallas_call(
        _s5_pre_kernel,
        grid=(t // tm,),
        in_specs=[pl.BlockSpec((tm, d), lambda i: (i, 0)),
                  pl.BlockSpec((1, d), lambda i: (0, 0)),
                  pl.BlockSpec((None, 1, d), lambda i: (i // per_b, 0, 0)),
                  pl.BlockSpec((None, 1, d), lambda i: (i // per_b, 0, 0))],
        out_specs=pl.BlockSpec((nt, tm // CHUNK, CHUNK * LANES), lambda i: (0, i, 0)),
        out_shape=jax.ShapeDtypeStruct((nt, t // CHUNK, CHUNK * LANES), BF16),
        scratch_shapes=[pltpu.VMEM((nt, tm, LANES), F32)],
        compiler_params=_cparams("parallel"),
        name="s5_pre",
    )(x2, g, shift, scale)


def _s5_state_kernel(h_ref, m3_ref, o_ref):
    o_ref[...] = _dot(h_ref[...], m3_ref[...])


def _s5_state(ho, m3, batch):
    no, r, kc = ho.shape
    c = r // batch
    return pl.pallas_call(
        _s5_state_kernel,
        grid=(no, batch),
        in_specs=[pl.BlockSpec((None, c, kc), lambda j, b: (j, b, 0)),
                  pl.BlockSpec((None, kc, OCT_STATE), lambda j, b: (j, 0, 0))],
        out_specs=pl.BlockSpec((None, c, OCT_STATE), lambda j, b: (j, 0, b)),
        out_shape=jax.ShapeDtypeStruct((no, c, batch * OCT_STATE), F32),
        compiler_params=_cparams("parallel", "parallel"),
        name="s5_state",
    )(ho, m3)


def _s5_scan_kernel(v_ref, a1_ref, a2_ref, o_ref, *, batch, unroll):
    a1 = a1_ref[...]
    a2 = a2_ref[...]
    half = OCT_STATE // 2

    def swap(s):
        parts = []
        for b in range(batch):
            parts.append(s[:, b * OCT_STATE + half:(b + 1) * OCT_STATE])
            parts.append(s[:, b * OCT_STATE:b * OCT_STATE + half])
        return jnp.concatenate(parts, axis=-1)

    def body(i, s):
        base = pl.multiple_of(i * unroll, unroll)
        v = v_ref[pl.ds(base, unroll), :]
        rows = []
        for k in range(unroll):
            rows.append(s)
            s = a1 * s + a2 * swap(s) + v[k:k + 1, :]
        o_ref[pl.ds(base, unroll), :] = jnp.concatenate(rows, axis=0).astype(o_ref.dtype)
        return s

    lax.fori_loop(0, v_ref.shape[0] // unroll, body, jnp.zeros((1, v_ref.shape[1]), F32))


def _s5_scan(v, a1, a2, batch, unroll=16):
    no, c, w = v.shape
    return pl.pallas_call(
        functools.partial(_s5_scan_kernel, batch=batch, unroll=unroll),
        grid=(no,),
        in_specs=[pl.BlockSpec((None, c, w), lambda j: (j, 0, 0)),
                  pl.BlockSpec((None, 1, w), lambda j: (j, 0, 0)),
                  pl.BlockSpec((None, 1, w), lambda j: (j, 0, 0))],
        out_specs=pl.BlockSpec((None, c, w), lambda j: (j, 0, 0)),
        out_shape=jax.ShapeDtypeStruct((no, c, w), BF16),
        compiler_params=_cparams("parallel"),
        name="s5_scan",
    )(v, a1, a2)


def _s5_out_kernel(h_ref, s_ref, m1_ref, m2_ref, d_ref, o_ref):
    tw = m1_ref.shape[1]
    n_tiles = h_ref.shape[1] // tw
    for tile in range(n_tiles):
        cols = slice(tile * tw, (tile + 1) * tw)
        y = (_dot(h_ref[:, :(tile + 1) * tw], m1_ref[(n_tiles - 1 - tile) * tw:, :])
             + _dot(s_ref[...], m2_ref[:, cols])
             + d_ref[:, cols] * h_ref[:, cols].astype(F32))
        o_ref[:, cols] = jax.nn.gelu(y).astype(BF16)


def _s5_out(ho, sin, m1, m2, dsk, batch):
    no, r, kc = ho.shape
    c = r // batch
    return pl.pallas_call(
        _s5_out_kernel,
        grid=(no, batch),
        in_specs=[pl.BlockSpec((None, c, kc), lambda j, b: (j, b, 0)),
                  pl.BlockSpec((None, c, OCT_STATE), lambda j, b: (j, 0, b)),
                  pl.BlockSpec((None, kc, m1.shape[2]), lambda j, b: (j, 0, 0)),
                  pl.BlockSpec((None, OCT_STATE, kc), lambda j, b: (j, 0, 0)),
                  pl.BlockSpec((None, 1, kc), lambda j, b: (j, 0, 0))],
        out_specs=pl.BlockSpec((None, c, kc), lambda j, b: (j, b, 0)),
        out_shape=jax.ShapeDtypeStruct((no, r, kc), BF16),
        compiler_params=_cparams("parallel", "parallel"),
        name="s5_out",
    )(ho, sin, m1, m2, dsk)


def _s5_glu_kernel(y_ref, x_ref, w_ref, b_ref, g_ref, gate_ref, o_ref, ybuf, slab):
    nt, rows, _ = y_ref.shape
    for t in range(CHUNK):
        for j in range(nt):
            ybuf[t * rows:(t + 1) * rows, j * LANES:(j + 1) * LANES] = y_ref[j, :, t * LANES:(t + 1) * LANES]
    y = ybuf[...]
    z = _dot(y, w_ref[...]) + b_ref[...]
    m = y.astype(F32) * jax.nn.sigmoid(z)
    res = (1.0 + gate_ref[...]) * (_rms(m) * g_ref[...])
    for j in range(nt):
        for t in range(CHUNK):
            slab.at[j][pl.ds(t, rows, stride=CHUNK), :] = res[t * rows:(t + 1) * rows, j * LANES:(j + 1) * LANES]
    for j in range(nt):
        o_ref[:, j * LANES:(j + 1) * LANES] = x_ref[:, j * LANES:(j + 1) * LANES] + slab[j]


def _s5_glu(yo, x2, w, b, g, gate, seq, tm=512):
    t, d = x2.shape
    nt = d // LANES
    per_b = seq // tm
    return pl.pallas_call(
        _s5_glu_kernel,
        grid=(t // tm,),
        in_specs=[pl.BlockSpec((nt, tm // CHUNK, CHUNK * LANES), lambda i: (0, i, 0)),
                  pl.BlockSpec((tm, d), lambda i: (i, 0)),
                  pl.BlockSpec((d, d), lambda i: (0, 0)),
                  pl.BlockSpec((1, d), lambda i: (0, 0)),
                  pl.BlockSpec((1, d), lambda i: (0, 0)),
                  pl.BlockSpec((None, 1, d), lambda i: (i // per_b, 0, 0))],
        out_specs=pl.BlockSpec((tm, d), lambda i: (i, 0)),
        out_shape=jax.ShapeDtypeStruct((t, d), F32),
        scratch_shapes=[pltpu.VMEM((tm, d), BF16), pltpu.VMEM((nt, tm, LANES), F32)],
        compiler_params=_cparams("parallel"),
        name="s5_glu",
    )(yo, x2, w, b, g, gate)


FFN_TF = 512
FFN_TN = 256


def _swiglu_hidden(h, wg_ref, wu_ref, hid_ref):
    for c in range(hid_ref.shape[1] // LANES):
        cols = slice(c * LANES, (c + 1) * LANES)
        r = _dot(h, jnp.concatenate([wg_ref[:, cols], wu_ref[:, cols]], axis=1))
        a, u = r[:, :LANES], r[:, LANES:]
        hid_ref[:, cols] = (a * jax.nn.sigmoid(a) * u).astype(BF16)


def _swiglu_down(hid, wd_ref):
    nf, _, tf = hid.shape
    out = _dot(hid[0], wd_ref[0:tf, :])
    for c in range(1, nf):
        out = out + _dot(hid[c], wd_ref[c * tf:(c + 1) * tf, :])
    return out


def _ffn_kernel(x_ref, g_ref, sh_ref, sc_ref, wg_ref, wu_ref, wd_ref, gp_ref, gate_ref, o_ref, h_scr, hid, acc):
    s = pl.program_id(1)
    nf = hid.shape[0]
    nn = acc.shape[0]

    @pl.when(s == 0)
    def _():
        h_scr[...] = _norm_mod(x_ref[...], g_ref[...], sh_ref[...], sc_ref[...]).astype(BF16)

    @pl.when(s < nf)
    def _():
        _swiglu_hidden(h_scr[...], wg_ref, wu_ref, hid.at[jnp.minimum(s, nf - 1)])

    @pl.when(s >= nf)
    def _():
        acc[jnp.maximum(s - nf, 0)] = _swiglu_down(hid, wd_ref)

    @pl.when(s == nf + nn - 1)
    def _():
        f = jnp.concatenate([acc[n] for n in range(nn)], axis=1)
        o_ref[...] = x_ref[...] + (1.0 + gate_ref[...]) * (_rms(f) * gp_ref[...])


def _ffn(x2, g, shift, scale, wg, wu, wd, gp, gate, seq, tm=512):
    t, d = x2.shape
    f = wd.shape[0]
    nf, nn = f // FFN_TF, d // FFN_TN
    per_b = seq // tm
    mod = pl.BlockSpec((None, 1, d), lambda i, s: (i // per_b, 0, 0))
    row = pl.BlockSpec((1, d), lambda i, s: (0, 0))
    wcol = pl.BlockSpec((d, FFN_TF), lambda i, s: (0, jnp.minimum(s, nf - 1)))
    return pl.pallas_call(
        _ffn_kernel,
        grid=(t // tm, nf + nn),
        in_specs=[pl.BlockSpec((tm, d), lambda i, s: (i, 0)), row, mod, mod, wcol, wcol,
                  pl.BlockSpec((f, FFN_TN), lambda i, s: (0, jnp.maximum(s - nf, 0))),
                  row, mod],
        out_specs=pl.BlockSpec((tm, d), lambda i, s: (i, 0)),
        out_shape=jax.ShapeDtypeStruct((t, d), F32),
        scratch_shapes=[pltpu.VMEM((tm, d), BF16), pltpu.VMEM((nf, tm, FFN_TF), BF16),
                        pltpu.VMEM((nn, tm, FFN_TN), F32)],
        compiler_params=_cparams("parallel", "arbitrary"),
        name="dense_ffn",
    )(x2, g, shift, scale, wg, wu, wd, gp, gate)


def _qkv_kernel(x_ref, gq_ref, shq_ref, scq_ref, gkv_ref, shkv_ref, sckv_ref, w_ref, o_ref, hq, hkv, *, qscale):
    n = pl.program_id(1)

    @pl.when(n == 0)
    def _():
        xf = x_ref[...]
        xn = _rms(xf)
        hq[...] = (xn * (gq_ref[...] * (1.0 + scq_ref[...])) + shq_ref[...]).astype(BF16)
        hkv[...] = (xn * (gkv_ref[...] * (1.0 + sckv_ref[...])) + shkv_ref[...]).astype(BF16)
        o_ref[...] = (_dot(hq[...], w_ref[...]) * qscale).astype(BF16)

    @pl.when(n > 0)
    def _():
        o_ref[...] = _dot(hkv[...], w_ref[...]).astype(BF16)


def _qkv(x2, gq, shq, scq, gkv, shkv, sckv, w3, seq, tm=512):
    t, d = x2.shape
    per_b = seq // tm
    mod = pl.BlockSpec((None, 1, d), lambda i, n: (i // per_b, 0, 0))
    row = pl.BlockSpec((1, d), lambda i, n: (0, 0))
    qscale = 1.0 / math.sqrt(d // N_HEADS)
    return pl.pallas_call(
        functools.partial(_qkv_kernel, qscale=qscale),
        grid=(t // tm, 3),
        in_specs=[pl.BlockSpec((tm, d), lambda i, n: (i, 0)), row, mod, mod, row, mod, mod,
                  pl.BlockSpec((None, d, d), lambda i, n: (n, 0, 0))],
        out_specs=pl.BlockSpec((None, tm, d), lambda i, n: (n, i, 0)),
        out_shape=jax.ShapeDtypeStruct((3, t, d), BF16),
        scratch_shapes=[pltpu.VMEM((tm, d), BF16), pltpu.VMEM((tm, d), BF16)],
        compiler_params=_cparams("parallel", "arbitrary"),
        name="qkv_proj",
    )(x2, gq, shq, scq, gkv, shkv, sckv, w3)


EXP_ZERO_BELOW = -104.0


def _attn_kernel(q_ref, k_ref, v_ref, o_ref, *, tb, nh, hd):
    qi = pl.program_id(2)
    row = lax.broadcasted_iota(jnp.int32, (tb, tb), 0)
    col = lax.broadcasted_iota(jnp.int32, (tb, tb), 1)
    later = jnp.where(row > col, 1.0, 0.0).astype(BF16)
    valid = col < row

    def block(kb, accs, runs, diagonal):
        start = pl.multiple_of(kb * tb, tb)
        heads = range(nh)
        lanes = [slice(g * hd, (g + 1) * hd) for g in heads]
        zs = [lax.dot_general(q_ref[:, lanes[g]], k_ref[pl.ds(start, tb), lanes[g]], (((1,), (1,)), ((), ())),
                              preferred_element_type=F32) for g in heads]
        sps = [jnp.maximum(z, 0.0) + jnp.log(1.0 + jnp.exp(-jnp.abs(z))) for z in zs]
        l1ms = [jnp.where(valid, -sp, 0.0) if diagonal else -sp for sp in sps]
        his = [l1m.astype(BF16) for l1m in l1ms]
        los = [(l1m - hi.astype(F32)).astype(BF16) for l1m, hi in zip(l1ms, his)]
        suffixes = [_dot(hi, later) + _dot(lo, later) for hi, lo in zip(his, los)]
        ws = [jnp.exp((zs[g] - sps[g]) + suffixes[g] + runs[g]) for g in heads]
        if diagonal:
            ws = [jnp.where(valid, w, 0.0) for w in ws]
        new_accs = tuple(accs[g] + _dot(ws[g].astype(BF16), v_ref[pl.ds(start, tb), lanes[g]]) for g in heads)
        new_runs = tuple(runs[g] + suffixes[g][:, 0:1] + l1ms[g][:, 0:1] for g in heads)
        return new_accs, new_runs

    accs = tuple(jnp.zeros((tb, hd), F32) for _ in range(nh))
    runs = tuple(jnp.zeros((tb, 1), F32) for _ in range(nh))
    accs, runs = block(qi, accs, runs, True)

    def cond(c):
        kb, _, runs = c
        top = runs[0]
        for g in range(1, nh):
            top = jnp.maximum(top, runs[g])
        return jnp.logical_and(kb >= 0, jnp.max(top) >= EXP_ZERO_BELOW)

    def body(c):
        kb, accs, runs = c
        accs, runs = block(kb, accs, runs, False)
        return kb - 1, accs, runs

    _, accs, _ = lax.while_loop(cond, body, (qi - 1, accs, runs))
    for g in range(nh):
        o_ref[:, g * hd:(g + 1) * hd] = accs[g].astype(o_ref.dtype)


def _attention(qkv, batch, seq, tb=128, nh=4):
    _, t, d = qkv.shape
    hd = d // N_HEADS
    nq = seq // tb
    return pl.pallas_call(
        functools.partial(_attn_kernel, tb=tb, nh=nh, hd=hd),
        grid=(batch, N_HEADS // nh, nq),
        in_specs=[pl.BlockSpec((None, tb, nh * hd), lambda b, h, i: (0, b * nq + i, h)),
                  pl.BlockSpec((None, seq, nh * hd), lambda b, h, i: (1, b, h)),
                  pl.BlockSpec((None, seq, nh * hd), lambda b, h, i: (2, b, h))],
        out_specs=pl.BlockSpec((tb, nh * hd), lambda b, h, i: (b * nq + i, h)),
        out_shape=jax.ShapeDtypeStruct((t, d), BF16),
        compiler_params=_cparams("parallel", "parallel", "parallel"),
        name="stickbreak_attn",
    )(qkv, qkv, qkv)


def _oproj_kernel(a_ref, x_ref, w_ref, g_ref, gate_ref, o_ref):
    m = _dot(a_ref[...], w_ref[...])
    o_ref[...] = x_ref[...] + (1.0 + gate_ref[...]) * (_rms(m) * g_ref[...])


def _oproj(a, x2, w, g, gate, seq, tm=512):
    t, d = x2.shape
    per_b = seq // tm
    return pl.pallas_call(
        _oproj_kernel,
        grid=(t // tm,),
        in_specs=[pl.BlockSpec((tm, d), lambda i: (i, 0)),
                  pl.BlockSpec((tm, d), lambda i: (i, 0)),
                  pl.BlockSpec((d, d), lambda i: (0, 0)),
                  pl.BlockSpec((1, d), lambda i: (0, 0)),
                  pl.BlockSpec((None, 1, d), lambda i: (i // per_b, 0, 0))],
        out_specs=pl.BlockSpec((tm, d), lambda i: (i, 0)),
        out_shape=jax.ShapeDtypeStruct((t, d), F32),
        compiler_params=_cparams("parallel"),
        name="attn_out_proj",
    )(a, x2, w, g, gate)


def _router_kernel(x_ref, g_ref, sh_ref, sc_ref, wr_ref, h_ref, r_ref):
    h = _norm_mod(x_ref[...], g_ref[...], sh_ref[...], sc_ref[...])
    h_ref[...] = h
    logits = jnp.dot(h, wr_ref[...], preferred_element_type=F32, precision=HIGHEST)
    lane = lax.broadcasted_iota(jnp.int32, logits.shape, 1).astype(F32)
    neg = jnp.float32(-jnp.inf)
    lg = jnp.where(lane < N_EXPERTS, logits, neg)
    m1 = jnp.max(lg, axis=-1, keepdims=True)
    i1 = jnp.min(jnp.where(lg == m1, lane, float(LANES)), axis=-1, keepdims=True)
    lg2 = jnp.where(lane == i1, neg, lg)
    m2 = jnp.max(lg2, axis=-1, keepdims=True)
    i2 = jnp.min(jnp.where(lg2 == m2, lane, float(LANES)), axis=-1, keepdims=True)
    e = jnp.exp(m2 - m1)
    p1 = 1.0 / (1.0 + e)
    p2 = e * p1
    r_ref[...] = jnp.where(lane == 0, i1, jnp.where(lane == 1, i2, jnp.where(lane == 2, p1, jnp.where(lane == 3, p2, 0.0))))


def _router(x2, g, shift, scale, wr_pad, seq, tm=512):
    t, d = x2.shape
    per_b = seq // tm
    mod = pl.BlockSpec((None, 1, d), lambda i: (i // per_b, 0, 0))
    return pl.pallas_call(
        _router_kernel,
        grid=(t // tm,),
        in_specs=[pl.BlockSpec((tm, d), lambda i: (i, 0)),
                  pl.BlockSpec((1, d), lambda i: (0, 0)), mod, mod,
                  pl.BlockSpec((d, LANES), lambda i: (0, 0))],
        out_specs=[pl.BlockSpec((tm, d), lambda i: (i, 0)),
                   pl.BlockSpec((tm, LANES), lambda i: (i, 0))],
        out_shape=[jax.ShapeDtypeStruct((t, d), F32), jax.ShapeDtypeStruct((t, LANES), F32)],
        compiler_params=_cparams("parallel"),
        name="moe_router",
    )(x2, g, shift, scale, wr_pad)


def _row_copy(src_hbm, dst, sem, src_row, dst_row):
    return pltpu.make_async_copy(src_hbm.at[pl.ds(src_row, 1)], dst.at[pl.ds(dst_row, 1)], sem)


def _moe_kernel(te_ref, nu_ref, tok_ref, nxt_ref, h_hbm, wg_ref, wu_ref, wd_ref, o_ref, xbuf, xb16, hid, sem):
    i = pl.program_id(0)
    j = pl.program_id(1)
    tm = xb16.shape[0]
    nf = hid.shape[0]
    used = i < nu_ref[0]
    slot = lax.rem(i, 2)

    def gather(idx_ref, s):
        def start(r, c):
            _row_copy(h_hbm, xbuf.at[s], sem.at[s], idx_ref[0, r], r).start()
            return c
        lax.fori_loop(0, tm, start, 0, unroll=8)

    @pl.when(jnp.logical_and(i == 0, j == 0))
    def _():
        gather(tok_ref, 0)

    @pl.when(jnp.logical_and(i + 1 < nu_ref[0], j == 0))
    def _():
        gather(nxt_ref, 1 - slot)

    @pl.when(jnp.logical_and(used, j == 0))
    def _():
        def wait(r, c):
            _row_copy(h_hbm, xbuf.at[slot], sem.at[slot], 0, r).wait()
            return c
        lax.fori_loop(0, tm, wait, 0, unroll=8)
        xb16[...] = xbuf[slot].astype(BF16)

    @pl.when(jnp.logical_and(used, j < nf))
    def _():
        _swiglu_hidden(xb16[...], wg_ref, wu_ref, hid.at[jnp.minimum(j, nf - 1)])

    @pl.when(jnp.logical_and(used, j >= nf))
    def _():
        o_ref[...] = _swiglu_down(hid, wd_ref)

    @pl.when(jnp.logical_and(jnp.logical_not(used), j >= nf))
    def _():
        o_ref[...] = jnp.zeros_like(o_ref)


def _moe_experts(tile_expert, n_used, sorted_tok, h, wg, wu, wd, tm=512):
    t, d = h.shape
    f = wd.shape[1]
    nt = sorted_tok.shape[0]
    nf, nn = f // FFN_TF, d // FFN_TN
    last = nf + nn - 1

    def wcol(i, j, te, nu):
        return (te[i], 0, jnp.minimum(jnp.where(i < nu[0], j, last), nf - 1))

    def wdown(i, j, te, nu):
        return (te[i], 0, jnp.maximum(jnp.where(i < nu[0], j, last) - nf, 0))

    grid_spec = pltpu.PrefetchScalarGridSpec(
        num_scalar_prefetch=2,
        grid=(nt, nf + nn),
        in_specs=[pl.BlockSpec((None, 1, tm), lambda i, j, te, nu: (i, 0, 0), memory_space=pltpu.SMEM),
                  pl.BlockSpec((None, 1, tm), lambda i, j, te, nu: (jnp.minimum(i + 1, nt - 1), 0, 0),
                               memory_space=pltpu.SMEM),
                  pl.BlockSpec(memory_space=pl.ANY),
                  pl.BlockSpec((None, d, FFN_TF), wcol),
                  pl.BlockSpec((None, d, FFN_TF), wcol),
                  pl.BlockSpec((None, f, FFN_TN), wdown)],
        out_specs=pl.BlockSpec((tm, FFN_TN), lambda i, j, te, nu: (i, jnp.maximum(j - nf, 0))),
        scratch_shapes=[pltpu.VMEM((2, tm, d), F32), pltpu.VMEM((tm, d), BF16), pltpu.VMEM((nf, tm, FFN_TF), BF16),
                        pltpu.SemaphoreType.DMA((2,))],
    )
    return pl.pallas_call(
        _moe_kernel,
        grid_spec=grid_spec,
        out_shape=jax.ShapeDtypeStruct((nt * tm, d), F32),
        compiler_params=_cparams("arbitrary", "arbitrary"),
        name="moe_experts",
    )(tile_expert, n_used, sorted_tok, sorted_tok, h, wg, wu, wd)


def _combine_kernel(d0_ref, d1_ref, n0_ref, n1_ref, y_hbm, r_ref, x_ref, g_ref, gate_ref, o_ref, buf, sem):
    i = pl.program_id(0)
    tm = x_ref.shape[0]
    slot = lax.rem(i, 2)

    def gather(i0_ref, i1_ref, s):
        def start(r, c):
            _row_copy(y_hbm, buf.at[s, 0], sem.at[s], i0_ref[0, r], r).start()
            _row_copy(y_hbm, buf.at[s, 1], sem.at[s], i1_ref[0, r], r).start()
            return c
        lax.fori_loop(0, tm, start, 0, unroll=8)

    @pl.when(i == 0)
    def _():
        gather(d0_ref, d1_ref, 0)

    @pl.when(i + 1 < pl.num_programs(0))
    def _():
        gather(n0_ref, n1_ref, 1 - slot)

    def wait(r, c):
        _row_copy(y_hbm, buf.at[slot, 0], sem.at[slot], 0, r).wait()
        _row_copy(y_hbm, buf.at[slot, 1], sem.at[slot], 0, r).wait()
        return c

    lax.fori_loop(0, tm, wait, 0, unroll=8)
    r = r_ref[...]
    f = r[:, 2:3] * buf[slot, 0] + r[:, 3:4] * buf[slot, 1]
    o_ref[...] = x_ref[...] + (1.0 + gate_ref[...]) * (_rms(f) * g_ref[...])


def _moe_combine(dest0, dest1, y, route, x2, g, gate, seq, tm=256):
    t, d = x2.shape
    per_b = seq // tm
    nt = t // tm
    idx = pl.BlockSpec((None, 1, tm), lambda i: (i, 0, 0), memory_space=pltpu.SMEM)
    nxt = pl.BlockSpec((None, 1, tm), lambda i: (jnp.minimum(i + 1, nt - 1), 0, 0), memory_space=pltpu.SMEM)
    d0 = dest0.reshape(nt, 1, tm)
    d1 = dest1.reshape(nt, 1, tm)
    return pl.pallas_call(
        _combine_kernel,
        grid=(nt,),
        in_specs=[idx, idx, nxt, nxt,
                  pl.BlockSpec(memory_space=pl.ANY),
                  pl.BlockSpec((tm, LANES), lambda i: (i, 0)),
                  pl.BlockSpec((tm, d), lambda i: (i, 0)),
                  pl.BlockSpec((1, d), lambda i: (0, 0)),
                  pl.BlockSpec((None, 1, d), lambda i: (i // per_b, 0, 0))],
        out_specs=pl.BlockSpec((tm, d), lambda i: (i, 0)),
        out_shape=jax.ShapeDtypeStruct((t, d), F32),
        scratch_shapes=[pltpu.VMEM((2, 2, tm, d), F32), pltpu.SemaphoreType.DMA((2,))],
        compiler_params=_cparams("arbitrary"),
        name="moe_combine",
    )(d0, d1, d0, d1, y, route, x2, g, gate)


def _routing_plan(route, tm):
    t = route.shape[0]
    ex = route[:, 0:2].astype(jnp.int32).T.reshape(-1)
    onehot = (ex[:, None] == jnp.arange(N_EXPERTS, dtype=jnp.int32)[None, :]).astype(jnp.int32)
    csum = jnp.cumsum(onehot, axis=0)
    rank = jnp.sum(onehot * csum, axis=1) - 1
    counts = csum[-1]
    tiles = (counts + tm - 1) // tm
    tile_end = jnp.cumsum(tiles)
    offset = (tile_end - tiles) * tm
    dest = offset[ex] + rank
    nt = (2 * t) // tm + N_EXPERTS
    n_used = tile_end[-1]
    tile_ids = jnp.minimum(jnp.arange(nt, dtype=jnp.int32), n_used - 1)
    tile_expert = jnp.sum((tile_ids[:, None] >= tile_end[None, :]).astype(jnp.int32), axis=1)
    token = jnp.arange(2 * t, dtype=jnp.int32) % t
    sorted_tok = jnp.zeros((nt * tm,), jnp.int32).at[dest].set(token)
    return (tile_expert.astype(jnp.int32), n_used.reshape(1).astype(jnp.int32),
            sorted_tok.reshape(nt, 1, tm), dest[:t], dest[t:])


def kernel(x, c, ada_mix_w, ada_mix_b, norm_mix_pre, norm_mix_post, ada_ffn_w, ada_ffn_b, norm_ffn_pre, norm_ffn_post, s5_a_re, s5_a_im, s5_b_re, s5_b_im, s5_c_re, s5_c_im, s5_d, s5_log_step, s5_glu_w, s5_glu_b, kv_ada_w, kv_ada_b, kv_norm, w_k, w_v, w_q, w_o, ffn_w_gate, ffn_w_up, ffn_w_down, moe_w_router, moe_w_gate, moe_w_up, moe_w_down):
    bsz, seq, d = x.shape
    x2 = x.reshape(bsz * seq, d)

    c_pad = jnp.zeros((8, d), F32).at[:bsz].set(c)
    mix = _ada(c_pad, ada_mix_w, ada_mix_b)
    ffn = _ada(c_pad, ada_ffn_w, ada_ffn_b)
    kvm = _ada(c_pad, kv_ada_w[None], kv_ada_b[None])

    def mods(m, layer, n):
        return [m[layer, :bsz, k * d:(k + 1) * d].reshape(bsz, 1, d) for k in range(n)]

    def row(v):
        return v.reshape(1, d)

    shift, scale, gate = mods(mix, 0, 3)
    m1, m2, m3, a1, a2, dsk = _s5_tables(s5_a_re[0], s5_a_im[0], s5_b_re[0], s5_b_im[0],
                                         s5_c_re[0], s5_c_im[0], s5_log_step[0], s5_d[0], bsz)
    ho = _s5_pre(x2, row(norm_mix_pre[0]), shift, scale, seq)
    v = _s5_state(ho, m3, bsz)
    sin = _s5_scan(v, a1, a2, bsz)
    yo = _s5_out(ho, sin, m1, m2, dsk, bsz)
    x2 = _s5_glu(yo, x2, s5_glu_w[0].astype(BF16), row(s5_glu_b[0]), row(norm_mix_post[0]), gate, seq)

    shift, scale, gate = mods(ffn, 0, 3)
    x2 = _ffn(x2, row(norm_ffn_pre[0]), shift, scale, ffn_w_gate[0].astype(BF16), ffn_w_up[0].astype(BF16),
              ffn_w_down[0].astype(BF16), row(norm_ffn_post[0]), gate, seq)

    shift, scale, gate = mods(mix, 1, 3)
    kv_shift, kv_scale = mods(kvm, 0, 2)
    w3 = jnp.stack([w_q[0], w_k, w_v]).astype(BF16)
    qkv = _qkv(x2, row(norm_mix_pre[1]), shift, scale, row(kv_norm), kv_shift, kv_scale, w3, seq)
    att = _attention(qkv, bsz, seq)
    x2 = _oproj(att, x2, w_o[0].astype(BF16), row(norm_mix_post[1]), gate, seq)

    shift, scale, gate = mods(ffn, 1, 3)
    wr_pad = jnp.zeros((d, LANES), F32).at[:, :N_EXPERTS].set(moe_w_router[0])
    h, route = _router(x2, row(norm_ffn_pre[1]), shift, scale, wr_pad, seq)
    tm_moe = 512
    tile_expert, n_used, sorted_tok, dest0, dest1 = _routing_plan(route, tm_moe)
    y = _moe_experts(tile_expert, n_used, sorted_tok, h, moe_w_gate[0].astype(BF16), moe_w_up[0].astype(BF16),
                     moe_w_down[0].astype(BF16), tm=tm_moe)
    x2 = _moe_combine(dest0, dest1, y, route, x2, row(norm_ffn_post[1]), gate, seq)
    return x2.reshape(bsz, seq, d)
```

```python
import functools
import math

import jax
import jax.numpy as jnp
from jax import lax
from jax.experimental import pallas as pl
from jax.experimental.pallas import tpu as pltpu

F32 = jnp.float32
BF16 = jnp.bfloat16

LANES = 128
VMEM_LIMIT = 56 * 1024 * 1024
RMS_EPS = 1e-6

S5_GROUP = 16
S5_STATE = 64
CHUNK = 16
OCT = LANES // S5_GROUP
OCT_STATE = 2 * OCT * S5_STATE
N_HEADS = 16
N_EXPERTS = 8
HIGHEST = lax.Precision.HIGHEST


def _cparams(*semantics):
    return pltpu.CompilerParams(dimension_semantics=semantics, vmem_limit_bytes=VMEM_LIMIT)


def _dot(a, b):
    return jnp.dot(a, b, preferred_element_type=F32)


def _rms(xf):
    ms = jnp.mean(xf * xf, axis=-1, keepdims=True)
    return xf * lax.rsqrt(ms + RMS_EPS)


def _norm_mod(xf, g, shift, scale):
    return _rms(xf) * (g * (1.0 + scale)) + shift


def _ada_kernel(c_ref, w_ref, b_ref, o_ref):
    s = c_ref[...]
    s = s * jax.nn.sigmoid(s)
    o_ref[...] = _dot(s.astype(BF16), w_ref[...].astype(BF16)) + b_ref[...]


def _ada(c_pad, w, b, tn=1024):
    nl, d, n = w.shape
    return pl.pallas_call(
        _ada_kernel,
        grid=(nl, n // tn),
        in_specs=[pl.BlockSpec((8, d), lambda l, j: (0, 0)),
                  pl.BlockSpec((None, d, tn), lambda l, j: (l, 0, j)),
                  pl.BlockSpec((None, 1, tn), lambda l, j: (l, 0, j))],
        out_specs=pl.BlockSpec((None, 8, tn), lambda l, j: (l, 0, j)),
        out_shape=jax.ShapeDtypeStruct((nl, 8, n), F32),
        compiler_params=_cparams("parallel", "parallel"),
        name="ada_mods",
    )(c_pad, w, b.reshape(nl, 1, n))


def _s5_tables(a_re, a_im, b_re, b_im, c_re, c_im, log_step, d_skip, batch):
    g, p = a_re.shape
    h, lc = S5_GROUP, CHUNK
    no = g // OCT
    dt = jnp.exp(log_step)[:, None]
    lr, li = a_re * dt, a_im * dt
    k = jnp.arange(lc + 1, dtype=F32)[:, None, None]
    mag = jnp.exp(k * lr[None])
    pw_r, pw_i = mag * jnp.cos(k * li[None]), mag * jnp.sin(k * li[None])
    nr, ni = pw_r[1] - 1.0, pw_i[1]
    den = a_re * a_re + a_im * a_im
    qr, qi = (nr * a_re + ni * a_im) / den, (ni * a_re - nr * a_im) / den
    bbr = qr[..., None] * b_re - qi[..., None] * b_im
    bbi = qr[..., None] * b_im + qi[..., None] * b_re
    same_group = jnp.eye(OCT, dtype=F32).reshape(1, 1, OCT, 1, 1, OCT, 1)

    wr, wi = pw_r[:lc, :, None, :], pw_i[:lc, :, None, :]
    cwr = c_re[None] * wr - c_im[None] * wi
    cwi = c_re[None] * wi + c_im[None] * wr
    kk = (jnp.einsum('tgop,gpi->tgio', cwr, bbr, precision=HIGHEST)
          - jnp.einsum('tgop,gpi->tgio', cwi, bbi, precision=HIGHEST))
    kk = kk.reshape(lc, no, OCT, h, h).transpose(1, 0, 2, 3, 4)
    kd = (kk[:, :, :, :, None, None, :] * same_group).reshape(no, lc, LANES, LANES)
    kd = jnp.concatenate([jnp.zeros((no, 1, LANES, LANES), F32), kd], axis=1)
    tiles = []
    for delta in range(lc // 2 - 1, -1, -1):
        top = jnp.concatenate([kd[:, 2 * delta + 1], kd[:, 2 * delta + 2]], axis=-1)
        bottom = jnp.concatenate([kd[:, 2 * delta], kd[:, 2 * delta + 1]], axis=-1)
        tiles += [top, bottom]
    m1 = jnp.concatenate(tiles, axis=-2)

    er = c_re[None] * pw_r[1:, :, None, :] - c_im[None] * pw_i[1:, :, None, :]
    ei = c_re[None] * pw_i[1:, :, None, :] + c_im[None] * pw_r[1:, :, None, :]
    e2 = jnp.stack([er, -ei]).reshape(2, lc, no, OCT, h, p).transpose(2, 0, 3, 5, 1, 4)
    m2 = (e2.astype(BF16)[:, :, :, :, :, None, :] * same_group.astype(BF16)).reshape(no, OCT_STATE, lc * LANES)

    fr = pw_r[lc - 1::-1][:lc, :, :, None] * bbr[None] - pw_i[lc - 1::-1][:lc, :, :, None] * bbi[None]
    fi = pw_r[lc - 1::-1][:lc, :, :, None] * bbi[None] + pw_i[lc - 1::-1][:lc, :, :, None] * bbr[None]
    f2 = jnp.stack([fr, fi]).reshape(2, lc, no, OCT, p, h).transpose(2, 1, 3, 5, 0, 4)
    m3 = (f2.astype(BF16)[:, :, :, :, :, None, :] * same_group.astype(BF16)).reshape(no, lc * LANES, OCT_STATE)

    ar = pw_r[lc].reshape(no, OCT * p)
    ai = pw_i[lc].reshape(no, OCT * p)
    a1 = jnp.tile(jnp.concatenate([ar, ar], axis=-1), (1, batch))[:, None, :]
    a2 = jnp.tile(jnp.concatenate([-ai, ai], axis=-1), (1, batch))[:, None, :]
    dsk = jnp.tile(d_skip.reshape(no, 1, LANES), (1, 1, lc))
    return m1.astype(BF16), m2.astype(BF16), m3.astype(BF16), a1, a2, dsk


def _s5_pre_kernel(x_ref, g_ref, sh_ref, sc_ref, o_ref, slab):
    tm = x_ref.shape[0]
    nt = x_ref.shape[1] // LANES
    rows = tm // CHUNK
    h = _norm_mod(x_ref[...], g_ref[...], sh_ref[...], sc_ref[...])
    for j in range(nt):
        slab[j] = h[:, j * LANES:(j + 1) * LANES]
    for j in range(nt):
        for t in range(CHUNK):
            o_ref[j, :, t * LANES:(t + 1) * LANES] = slab.at[j][pl.ds(t, rows, stride=CHUNK), :].astype(BF16)


def _s5_pre(x2, g, shift, scale, seq, tm=512):
    t, d = x2.shape
    nt = d // LANES
    per_b = seq // tm
    return pl.pallas_call(
        _s5_pre_kernel,
        grid=(t // tm,),
        in_specs=[pl.BlockSpec((tm, d), lambda i: (i, 0)),
                  pl.BlockSpec((1, d), lambda i: (0, 0)),
                  pl.BlockSpec((None, 1, d), lambda i: (i // per_b, 0, 0)),
                  pl.BlockSpec((None, 1, d), lambda i: (i // per_b, 0, 0))],
        out_specs=pl.BlockSpec((nt, tm // CHUNK, CHUNK * LANES), lambda i: (0, i, 0)),
        out_shape=jax.ShapeDtypeStruct((nt, t // CHUNK, CHUNK * LANES), BF16),
        scratch_shapes=[pltpu.VMEM((nt, tm, LANES), F32)],
        compiler_params=_cparams("parallel"),
        name="s5_pre",
    )(x2, g, shift, scale)


def _s5_state_kernel(h_ref, m3_ref, o_ref):
    o_ref[...] = _dot(h_ref[...], m3_ref[...])


def _s5_state(ho, m3, batch):
    no, r, kc = ho.shape
    c = r // batch
    return pl.pallas_call(
        _s5_state_kernel,
        grid=(no, batch),
        in_specs=[pl.BlockSpec((None, c, kc), lambda j, b: (j, b, 0)),
                  pl.BlockSpec((None, kc, OCT_STATE), lambda j, b: (j, 0, 0))],
        out_specs=pl.BlockSpec((None, c, OCT_STATE), lambda j, b: (j, 0, b)),
        out_shape=jax.ShapeDtypeStruct((no, c, batch * OCT_STATE), F32),
        compiler_params=_cparams("parallel", "parallel"),
        name="s5_state",
    )(ho, m3)


def _s5_scan_kernel(v_ref, a1_ref, a2_ref, o_ref, *, batch, unroll):
    a1 = a1_ref[...]
    a2 = a2_ref[...]
    half = OCT_STATE // 2

    def swap(s):
        parts = []
        for b in range(batch):
            parts.append(s[:, b * OCT_STATE + half:(b + 1) * OCT_STATE])
            parts.append(s[:, b * OCT_STATE:b * OCT_STATE + half])
        return jnp.concatenate(parts, axis=-1)

    def body(i, s):
        base = pl.multiple_of(i * unroll, unroll)
        v = v_ref[pl.ds(base, unroll), :]
        rows = []
        for k in range(unroll):
            rows.append(s)
            s = a1 * s + a2 * swap(s) + v[k:k + 1, :]
        o_ref[pl.ds(base, unroll), :] = jnp.concatenate(rows, axis=0).astype(o_ref.dtype)
        return s

    lax.fori_loop(0, v_ref.shape[0] // unroll, body, jnp.zeros((1, v_ref.shape[1]), F32))


def _s5_scan(v, a1, a2, batch, unroll=16):
    no, c, w = v.shape
    return pl.pallas_call(
        functools.partial(_s5_scan_kernel, batch=batch, unroll=unroll),
        grid=(no,),
        in_specs=[pl.BlockSpec((None, c, w), lambda j: (j, 0, 0)),
                  pl.BlockSpec((None, 1, w), lambda j: (j, 0, 0)),
                  pl.BlockSpec((None, 1, w), lambda j: (j, 0, 0))],
        out_specs=pl.BlockSpec((None, c, w), lambda j: (j, 0, 0)),
        out_shape=jax.ShapeDtypeStruct((no, c, w), BF16),
        compiler_params=_cparams("parallel"),
        name="s5_scan",
    )(v, a1, a2)


def _s5_out_kernel(h_ref, s_ref, m1_ref, m2_ref, d_ref, o_ref):
    tw = m1_ref.shape[1]
    n_tiles = h_ref.shape[1] // tw
    for tile in range(n_tiles):
        cols = slice(tile * tw, (tile + 1) * tw)
        y = (_dot(h_ref[:, :(tile + 1) * tw], m1_ref[(n_tiles - 1 - tile) * tw:, :])
             + _dot(s_ref[...], m2_ref[:, cols])
             + d_ref[:, cols] * h_ref[:, cols].astype(F32))
        o_ref[:, cols] = jax.nn.gelu(y).astype(BF16)


def _s5_out(ho, sin, m1, m2, dsk, batch):
    no, r, kc = ho.shape
    c = r // batch
    return pl.pallas_call(
        _s5_out_kernel,
        grid=(no, batch),
        in_specs=[pl.BlockSpec((None, c, kc), lambda j, b: (j, b, 0)),
                  pl.BlockSpec((None, c, OCT_STATE), lambda j, b: (j, 0, b)),
                  pl.BlockSpec((None, kc, m1.shape[2]), lambda j, b: (j, 0, 0)),
                  pl.BlockSpec((None, OCT_STATE, kc), lambda j, b: (j, 0, 0)),
                  pl.BlockSpec((None, 1, kc), lambda j, b: (j, 0, 0))],
        out_specs=pl.BlockSpec((None, c, kc), lambda j, b: (j, b, 0)),
        out_shape=jax.ShapeDtypeStruct((no, r, kc), BF16),
        compiler_params=_cparams("parallel", "parallel"),
        name="s5_out",
    )(ho, sin, m1, m2, dsk)


def _s5_glu_kernel(y_ref, x_ref, w_ref, b_ref, g_ref, gate_ref, o_ref, ybuf, slab):
    nt, rows, _ = y_ref.shape
    for t in range(CHUNK):
        for j in range(nt):
            ybuf[t * rows:(t + 1) * rows, j * LANES:(j + 1) * LANES] = y_ref[j, :, t * LANES:(t + 1) * LANES]
    y = ybuf[...]
    z = _dot(y, w_ref[...]) + b_ref[...]
    m = y.astype(F32) * jax.nn.sigmoid(z)
    res = (1.0 + gate_ref[...]) * (_rms(m) * g_ref[...])
    for j in range(nt):
        for t in range(CHUNK):
            slab.at[j][pl.ds(t, rows, stride=CHUNK), :] = res[t * rows:(t + 1) * rows, j * LANES:(j + 1) * LANES]
    for j in range(nt):
        o_ref[:, j * LANES:(j + 1) * LANES] = x_ref[:, j * LANES:(j + 1) * LANES] + slab[j]


def _s5_glu(yo, x2, w, b, g, gate, seq, tm=512):
    t, d = x2.shape
    nt = d // LANES
    per_b = seq // tm
    return pl.pallas_call(
        _s5_glu_kernel,
        grid=(t // tm,),
        in_specs=[pl.BlockSpec((nt, tm // CHUNK, CHUNK * LANES), lambda i: (0, i, 0)),
                  pl.BlockSpec((tm, d), lambda i: (i, 0)),
                  pl.BlockSpec((d, d), lambda i: (0, 0)),
                  pl.BlockSpec((1, d), lambda i: (0, 0)),
                  pl.BlockSpec((1, d), lambda i: (0, 0)),
                  pl.BlockSpec((None, 1, d), lambda i: (i // per_b, 0, 0))],
        out_specs=pl.BlockSpec((tm, d), lambda i: (i, 0)),
        out_shape=jax.ShapeDtypeStruct((t, d), F32),
        scratch_shapes=[pltpu.VMEM((tm, d), BF16), pltpu.VMEM((nt, tm, LANES), F32)],
        compiler_params=_cparams("parallel"),
        name="s5_glu",
    )(yo, x2, w, b, g, gate)


FFN_TF = 1024


def _swiglu_step(h, wg_ref, wu_ref, wd_ref, hid_scr):
    for c in range(hid_scr.shape[1] // LANES):
        cols = slice(c * LANES, (c + 1) * LANES)
        r = _dot(h, jnp.concatenate([wg_ref[:, cols], wu_ref[:, cols]], axis=1))
        a, u = r[:, :LANES], r[:, LANES:]
        hid_scr[:, cols] = (a * jax.nn.sigmoid(a) * u).astype(BF16)
    return _dot(hid_scr[...], wd_ref[...])


def _ffn_kernel(x_ref, g_ref, sh_ref, sc_ref, wg_ref, wu_ref, wd_ref, gp_ref, gate_ref, o_ref, h_scr, hid_scr):
    j = pl.program_id(1)

    @pl.when(j == 0)
    def _():
        h_scr[...] = _norm_mod(x_ref[...], g_ref[...], sh_ref[...], sc_ref[...]).astype(BF16)
        o_ref[...] = jnp.zeros_like(o_ref)

    o_ref[...] += _swiglu_step(h_scr[...], wg_ref, wu_ref, wd_ref, hid_scr)

    @pl.when(j == pl.num_programs(1) - 1)
    def _():
        o_ref[...] = x_ref[...] + (1.0 + gate_ref[...]) * (_rms(o_ref[...]) * gp_ref[...])


def _ffn(x2, g, shift, scale, wg, wu, wd, gp, gate, seq, tm=512):
    t, d = x2.shape
    f = wd.shape[0]
    per_b = seq // tm
    mod = pl.BlockSpec((None, 1, d), lambda i, j: (i // per_b, 0, 0))
    row = pl.BlockSpec((1, d), lambda i, j: (0, 0))
    wcol = pl.BlockSpec((d, FFN_TF), lambda i, j: (0, j))
    return pl.pallas_call(
        _ffn_kernel,
        grid=(t // tm, f // FFN_TF),
        in_specs=[pl.BlockSpec((tm, d), lambda i, j: (i, 0)), row, mod, mod, wcol, wcol,
                  pl.BlockSpec((FFN_TF, d), lambda i, j: (j, 0)),
                  row, mod],
        out_specs=pl.BlockSpec((tm, d), lambda i, j: (i, 0)),
        out_shape=jax.ShapeDtypeStruct((t, d), F32),
        scratch_shapes=[pltpu.VMEM((tm, d), BF16), pltpu.VMEM((tm, FFN_TF), BF16)],
        compiler_params=_cparams("parallel", "arbitrary"),
        name="dense_ffn",
    )(x2, g, shift, scale, wg, wu, wd, gp, gate)


def _qkv_kernel(x_ref, gq_ref, shq_ref, scq_ref, gkv_ref, shkv_ref, sckv_ref, w_ref, o_ref, hq, hkv, *, qscale):
    n = pl.program_id(1)

    @pl.when(n == 0)
    def _():
        xf = x_ref[...]
        xn = _rms(xf)
        hq[...] = (xn * (gq_ref[...] * (1.0 + scq_ref[...])) + shq_ref[...]).astype(BF16)
        hkv[...] = (xn * (gkv_ref[...] * (1.0 + sckv_ref[...])) + shkv_ref[...]).astype(BF16)
        o_ref[...] = (_dot(hq[...], w_ref[...]) * qscale).astype(BF16)

    @pl.when(n > 0)
    def _():
        o_ref[...] = _dot(hkv[...], w_ref[...]).astype(BF16)


def _qkv(x2, gq, shq, scq, gkv, shkv, sckv, w3, seq, tm=512):
    t, d = x2.shape
    per_b = seq // tm
    mod = pl.BlockSpec((None, 1, d), lambda i, n: (i // per_b, 0, 0))
    row = pl.BlockSpec((1, d), lambda i, n: (0, 0))
    qscale = 1.0 / math.sqrt(d // N_HEADS)
    return pl.pallas_call(
        functools.partial(_qkv_kernel, qscale=qscale),
        grid=(t // tm, 3),
        in_specs=[pl.BlockSpec((tm, d), lambda i, n: (i, 0)), row, mod, mod, row, mod, mod,
                  pl.BlockSpec((None, d, d), lambda i, n: (n, 0, 0))],
        out_specs=pl.BlockSpec((None, tm, d), lambda i, n: (n, i, 0)),
        out_shape=jax.ShapeDtypeStruct((3, t, d), BF16),
        scratch_shapes=[pltpu.VMEM((tm, d), BF16), pltpu.VMEM((tm, d), BF16)],
        compiler_params=_cparams("parallel", "arbitrary"),
        name="qkv_proj",
    )(x2, gq, shq, scq, gkv, shkv, sckv, w3)


EXP_ZERO_BELOW = -104.0


def _attn_kernel(q_ref, k_ref, v_ref, o_ref, *, tb, nh, hd):
    qi = pl.program_id(2)
    row = lax.broadcasted_iota(jnp.int32, (tb, tb), 0)
    col = lax.broadcasted_iota(jnp.int32, (tb, tb), 1)
    later = jnp.where(row > col, 1.0, 0.0).astype(BF16)
    valid = col < row

    heads = range(nh)
    lanes = [slice(g * hd, (g + 1) * hd) for g in heads]

    def walk(kbs, accs, runs, from_diagonal):
        pairs = [(b, g) for b in range(len(kbs)) for g in heads]
        ok = [kb >= 0 for kb in kbs]
        start = [pl.multiple_of(jnp.maximum(kb, 0) * tb, tb) for kb in kbs]
        z, sp, l1m, suffix = {}, {}, {}, {}
        for b, g in pairs:
            z[b, g] = lax.dot_general(q_ref[:, lanes[g]], k_ref[pl.ds(start[b], tb), lanes[g]],
                                      (((1,), (1,)), ((), ())), preferred_element_type=F32)
        for b, g in pairs:
            sp[b, g] = jnp.maximum(z[b, g], 0.0) + jnp.log(1.0 + jnp.exp(-jnp.abs(z[b, g])))
            keep = valid if (from_diagonal and b == 0) else ok[b]
            l1m[b, g] = jnp.where(keep, -sp[b, g], 0.0)
        for b, g in pairs:
            hi = l1m[b, g].astype(BF16)
            lo = (l1m[b, g] - hi.astype(F32)).astype(BF16)
            suffix[b, g] = _dot(hi, later) + _dot(lo, later)
        new_accs, new_runs = [], []
        for g in heads:
            run, acc = runs[g], accs[g]
            for b in range(len(kbs)):
                w = jnp.exp((z[b, g] - sp[b, g]) + suffix[b, g] + run)
                keep = valid if (from_diagonal and b == 0) else ok[b]
                w = jnp.where(keep, w, 0.0)
                acc = acc + _dot(w.astype(BF16), v_ref[pl.ds(start[b], tb), lanes[g]])
                run = run + suffix[b, g][:, 0:1] + l1m[b, g][:, 0:1]
            new_accs.append(acc)
            new_runs.append(run)
        return tuple(new_accs), tuple(new_runs)

    accs = tuple(jnp.zeros((tb, hd), F32) for _ in heads)
    runs = tuple(jnp.zeros((tb, 1), F32) for _ in heads)
    accs, runs = walk([qi, qi - 1, qi - 2], accs, runs, True)

    def cond(c):
        kb, _, runs = c
        top = runs[0]
        for g in range(1, nh):
            top = jnp.maximum(top, runs[g])
        return jnp.logical_and(kb >= 0, jnp.max(top) >= EXP_ZERO_BELOW)

    def body(c):
        kb, accs, runs = c
        accs, runs = walk([kb, kb - 1], accs, runs, False)
        return kb - 2, accs, runs

    _, accs, _ = lax.while_loop(cond, body, (qi - 3, accs, runs))
    for g in range(nh):
        o_ref[:, g * hd:(g + 1) * hd] = accs[g].astype(o_ref.dtype)


def _attention(qkv, batch, seq, tb=128, nh=4):
    _, t, d = qkv.shape
    hd = d // N_HEADS
    nq = seq // tb
    return pl.pallas_call(
        functools.partial(_attn_kernel, tb=tb, nh=nh, hd=hd),
        grid=(batch, N_HEADS // nh, nq),
        in_specs=[pl.BlockSpec((None, tb, nh * hd), lambda b, h, i: (0, b * nq + i, h)),
                  pl.BlockSpec((None, seq, nh * hd), lambda b, h, i: (1, b, h)),
                  pl.BlockSpec((None, seq, nh * hd), lambda b, h, i: (2, b, h))],
        out_specs=pl.BlockSpec((tb, nh * hd), lambda b, h, i: (b * nq + i, h)),
        out_shape=jax.ShapeDtypeStruct((t, d), BF16),
        compiler_params=_cparams("parallel", "parallel", "parallel"),
        name="stickbreak_attn",
    )(qkv, qkv, qkv)


def _oproj_kernel(a_ref, x_ref, w_ref, g_ref, gate_ref, o_ref):
    m = _dot(a_ref[...], w_ref[...])
    o_ref[...] = x_ref[...] + (1.0 + gate_ref[...]) * (_rms(m) * g_ref[...])


def _oproj(a, x2, w, g, gate, seq, tm=512):
    t, d = x2.shape
    per_b = seq // tm
    return pl.pallas_call(
        _oproj_kernel,
        grid=(t // tm,),
        in_specs=[pl.BlockSpec((tm, d), lambda i: (i, 0)),
                  pl.BlockSpec((tm, d), lambda i: (i, 0)),
                  pl.BlockSpec((d, d), lambda i: (0, 0)),
                  pl.BlockSpec((1, d), lambda i: (0, 0)),
                  pl.BlockSpec((None, 1, d), lambda i: (i // per_b, 0, 0))],
        out_specs=pl.BlockSpec((tm, d), lambda i: (i, 0)),
        out_shape=jax.ShapeDtypeStruct((t, d), F32),
        compiler_params=_cparams("parallel"),
        name="attn_out_proj",
    )(a, x2, w, g, gate)


def _router_kernel(x_ref, g_ref, sh_ref, sc_ref, wr_ref, h_ref, r_ref):
    h = _norm_mod(x_ref[...], g_ref[...], sh_ref[...], sc_ref[...])
    h_ref[...] = h
    logits = jnp.dot(h, wr_ref[...], preferred_element_type=F32, precision=HIGHEST)
    lane = lax.broadcasted_iota(jnp.int32, logits.shape, 1).astype(F32)
    neg = jnp.float32(-jnp.inf)
    lg = jnp.where(lane < N_EXPERTS, logits, neg)
    m1 = jnp.max(lg, axis=-1, keepdims=True)
    i1 = jnp.min(jnp.where(lg == m1, lane, float(LANES)), axis=-1, keepdims=True)
    lg2 = jnp.where(lane == i1, neg, lg)
    m2 = jnp.max(lg2, axis=-1, keepdims=True)
    i2 = jnp.min(jnp.where(lg2 == m2, lane, float(LANES)), axis=-1, keepdims=True)
    e = jnp.exp(m2 - m1)
    p1 = 1.0 / (1.0 + e)
    p2 = e * p1
    r_ref[...] = jnp.where(lane == 0, i1, jnp.where(lane == 1, i2, jnp.where(lane == 2, p1, jnp.where(lane == 3, p2, 0.0))))


def _router(x2, g, shift, scale, wr_pad, seq, tm=512):
    t, d = x2.shape
    per_b = seq // tm
    mod = pl.BlockSpec((None, 1, d), lambda i: (i // per_b, 0, 0))
    return pl.pallas_call(
        _router_kernel,
        grid=(t // tm,),
        in_specs=[pl.BlockSpec((tm, d), lambda i: (i, 0)),
                  pl.BlockSpec((1, d), lambda i: (0, 0)), mod, mod,
                  pl.BlockSpec((d, LANES), lambda i: (0, 0))],
        out_specs=[pl.BlockSpec((tm, d), lambda i: (i, 0)),
                   pl.BlockSpec((tm, LANES), lambda i: (i, 0))],
        out_shape=[jax.ShapeDtypeStruct((t, d), F32), jax.ShapeDtypeStruct((t, LANES), F32)],
        compiler_params=_cparams("parallel"),
        name="moe_router",
    )(x2, g, shift, scale, wr_pad)


def _row_copy(src_hbm, dst, sem, src_row, dst_row):
    return pltpu.make_async_copy(src_hbm.at[pl.ds(src_row, 1)], dst.at[pl.ds(dst_row, 1)], sem)


def _moe_kernel(te_ref, nu_ref, tok_ref, nxt_ref, h_hbm, wg_ref, wu_ref, wd_ref, o_ref, xbuf, xb16, hid_scr, sem):
    i = pl.program_id(0)
    j = pl.program_id(1)
    tm = xb16.shape[0]
    used = i < nu_ref[0]
    slot = lax.rem(i, 2)

    def gather(idx_ref, s):
        def start(r, c):
            _row_copy(h_hbm, xbuf.at[s], sem.at[s], idx_ref[0, r], r).start()
            return c
        lax.fori_loop(0, tm, start, 0, unroll=8)

    @pl.when(jnp.logical_and(i == 0, j == 0))
    def _():
        gather(tok_ref, 0)

    @pl.when(jnp.logical_and(i + 1 < nu_ref[0], j == 0))
    def _():
        gather(nxt_ref, 1 - slot)

    @pl.when(jnp.logical_and(used, j == 0))
    def _():
        def wait(r, c):
            _row_copy(h_hbm, xbuf.at[slot], sem.at[slot], 0, r).wait()
            return c
        lax.fori_loop(0, tm, wait, 0, unroll=8)
        xb16[...] = xbuf[slot].astype(BF16)

    @pl.when(j == 0)
    def _():
        o_ref[...] = jnp.zeros_like(o_ref)

    @pl.when(used)
    def _():
        o_ref[...] += _swiglu_step(xb16[...], wg_ref, wu_ref, wd_ref, hid_scr)


def _moe_experts(tile_expert, n_used, sorted_tok, h, wg, wu, wd, tm=512):
    t, d = h.shape
    f = wd.shape[1]
    nt = sorted_tok.shape[0]
    nf = f // FFN_TF

    def wcol(i, j, te, nu):
        return (te[i], 0, jnp.where(i < nu[0], j, nf - 1))

    def wrow(i, j, te, nu):
        return (te[i], jnp.where(i < nu[0], j, nf - 1), 0)

    grid_spec = pltpu.PrefetchScalarGridSpec(
        num_scalar_prefetch=2,
        grid=(nt, nf),
        in_specs=[pl.BlockSpec((None, 1, tm), lambda i, j, te, nu: (i, 0, 0), memory_space=pltpu.SMEM),
                  pl.BlockSpec((None, 1, tm), lambda i, j, te, nu: (jnp.minimum(i + 1, nt - 1), 0, 0),
                               memory_space=pltpu.SMEM),
                  pl.BlockSpec(memory_space=pl.ANY),
                  pl.BlockSpec((None, d, FFN_TF), wcol),
                  pl.BlockSpec((None, d, FFN_TF), wcol),
                  pl.BlockSpec((None, FFN_TF, d), wrow)],
        out_specs=pl.BlockSpec((tm, d), lambda i, j, te, nu: (i, 0)),
        scratch_shapes=[pltpu.VMEM((2, tm, d), F32), pltpu.VMEM((tm, d), BF16), pltpu.VMEM((tm, FFN_TF), BF16),
                        pltpu.SemaphoreType.DMA((2,))],
    )
    return pl.pallas_call(
        _moe_kernel,
        grid_spec=grid_spec,
        out_shape=jax.ShapeDtypeStruct((nt * tm, d), F32),
        compiler_params=_cparams("arbitrary", "arbitrary"),
        name="moe_experts",
    )(tile_expert, n_used, sorted_tok, sorted_tok, h, wg, wu, wd)


def _combine_kernel(d0_ref, d1_ref, n0_ref, n1_ref, y_hbm, r_ref, x_ref, g_ref, gate_ref, o_ref, buf, sem):
    i = pl.program_id(0)
    tm = x_ref.shape[0]
    slot = lax.rem(i, 2)

    def gather(i0_ref, i1_ref, s):
        def start(r, c):
            _row_copy(y_hbm, buf.at[s, 0], sem.at[s], i0_ref[0, r], r).start()
            _row_copy(y_hbm, buf.at[s, 1], sem.at[s], i1_ref[0, r], r).start()
            return c
        lax.fori_loop(0, tm, start, 0, unroll=8)

    @pl.when(i == 0)
    def _():
        gather(d0_ref, d1_ref, 0)

    @pl.when(i + 1 < pl.num_programs(0))
    def _():
        gather(n0_ref, n1_ref, 1 - slot)

    def wait(r, c):
        _row_copy(y_hbm, buf.at[slot, 0], sem.at[slot], 0, r).wait()
        _row_copy(y_hbm, buf.at[slot, 1], sem.at[slot], 0, r).wait()
        return c

    lax.fori_loop(0, tm, wait, 0, unroll=8)
    r = r_ref[...]
    f = r[:, 2:3] * buf[slot, 0] + r[:, 3:4] * buf[slot, 1]
    o_ref[...] = x_ref[...] + (1.0 + gate_ref[...]) * (_rms(f) * g_ref[...])


def _moe_combine(dest0, dest1, y, route, x2, g, gate, seq, tm=256):
    t, d = x2.shape
    per_b = seq // tm
    nt = t // tm
    idx = pl.BlockSpec((None, 1, tm), lambda i: (i, 0, 0), memory_space=pltpu.SMEM)
    nxt = pl.BlockSpec((None, 1, tm), lambda i: (jnp.minimum(i + 1, nt - 1), 0, 0), memory_space=pltpu.SMEM)
    d0 = dest0.reshape(nt, 1, tm)
    d1 = dest1.reshape(nt, 1, tm)
    return pl.pallas_call(
        _combine_kernel,
        grid=(nt,),
        in_specs=[idx, idx, nxt, nxt,
                  pl.BlockSpec(memory_space=pl.ANY),
                  pl.BlockSpec((tm, LANES), lambda i: (i, 0)),
                  pl.BlockSpec((tm, d), lambda i: (i, 0)),
                  pl.BlockSpec((1, d), lambda i: (0, 0)),
                  pl.BlockSpec((None, 1, d), lambda i: (i // per_b, 0, 0))],
        out_specs=pl.BlockSpec((tm, d), lambda i: (i, 0)),
        out_shape=jax.ShapeDtypeStruct((t, d), F32),
        scratch_shapes=[pltpu.VMEM((2, 2, tm, d), F32), pltpu.SemaphoreType.DMA((2,))],
        compiler_params=_cparams("arbitrary"),
        name="moe_combine",
    )(d0, d1, d0, d1, y, route, x2, g, gate)


def _routing_plan(route, tm):
    t = route.shape[0]
    ex = route[:, 0:2].astype(jnp.int32).T.reshape(-1)
    onehot = (ex[:, None] == jnp.arange(N_EXPERTS, dtype=jnp.int32)[None, :]).astype(jnp.int32)
    csum = jnp.cumsum(onehot, axis=0)
    rank = jnp.sum(onehot * csum, axis=1) - 1
    counts = csum[-1]
    tiles = (counts + tm - 1) // tm
    tile_end = jnp.cumsum(tiles)
    offset = (tile_end - tiles) * tm
    dest = offset[ex] + rank
    nt = (2 * t) // tm + N_EXPERTS
    n_used = tile_end[-1]
    tile_ids = jnp.minimum(jnp.arange(nt, dtype=jnp.int32), n_used - 1)
    tile_expert = jnp.sum((tile_ids[:, None] >= tile_end[None, :]).astype(jnp.int32), axis=1)
    token = jnp.arange(2 * t, dtype=jnp.int32) % t
    sorted_tok = jnp.zeros((nt * tm,), jnp.int32).at[dest].set(token)
    return (tile_expert.astype(jnp.int32), n_used.reshape(1).astype(jnp.int32),
            sorted_tok.reshape(nt, 1, tm), dest[:t], dest[t:])


def kernel(x, c, ada_mix_w, ada_mix_b, norm_mix_pre, norm_mix_post, ada_ffn_w, ada_ffn_b, norm_ffn_pre, norm_ffn_post, s5_a_re, s5_a_im, s5_b_re, s5_b_im, s5_c_re, s5_c_im, s5_d, s5_log_step, s5_glu_w, s5_glu_b, kv_ada_w, kv_ada_b, kv_norm, w_k, w_v, w_q, w_o, ffn_w_gate, ffn_w_up, ffn_w_down, moe_w_router, moe_w_gate, moe_w_up, moe_w_down):
    bsz, seq, d = x.shape
    x2 = x.reshape(bsz * seq, d)

    c_pad = jnp.zeros((8, d), F32).at[:bsz].set(c)
    mix = _ada(c_pad, ada_mix_w, ada_mix_b)
    ffn = _ada(c_pad, ada_ffn_w, ada_ffn_b)
    kvm = _ada(c_pad, kv_ada_w[None], kv_ada_b[None])

    def mods(m, layer, n):
        return [m[layer, :bsz, k * d:(k + 1) * d].reshape(bsz, 1, d) for k in range(n)]

    def row(v):
        return v.reshape(1, d)

    shift, scale, gate = mods(mix, 0, 3)
    m1, m2, m3, a1, a2, dsk = _s5_tables(s5_a_re[0], s5_a_im[0], s5_b_re[0], s5_b_im[0],
                                         s5_c_re[0], s5_c_im[0], s5_log_step[0], s5_d[0], bsz)
    ho = _s5_pre(x2, row(norm_mix_pre[0]), shift, scale, seq)
    v = _s5_state(ho, m3, bsz)
    sin = _s5_scan(v, a1, a2, bsz)
    yo = _s5_out(ho, sin, m1, m2, dsk, bsz)
    x2 = _s5_glu(yo, x2, s5_glu_w[0].astype(BF16), row(s5_glu_b[0]), row(norm_mix_post[0]), gate, seq)

    shift, scale, gate = mods(ffn, 0, 3)
    x2 = _ffn(x2, row(norm_ffn_pre[0]), shift, scale, ffn_w_gate[0].astype(BF16), ffn_w_up[0].astype(BF16),
              ffn_w_down[0].astype(BF16), row(norm_ffn_post[0]), gate, seq)

    shift, scale, gate = mods(mix, 1, 3)
    kv_shift, kv_scale = mods(kvm, 0, 2)
    w3 = jnp.stack([w_q[0], w_k, w_v]).astype(BF16)
    qkv = _qkv(x2, row(norm_mix_pre[1]), shift, scale, row(kv_norm), kv_shift, kv_scale, w3, seq)
    att = _attention(qkv, bsz, seq)
    x2 = _oproj(att, x2, w_o[0].astype(BF16), row(norm_mix_post[1]), gate, seq)

    shift, scale, gate = mods(ffn, 1, 3)
    wr_pad = jnp.zeros((d, LANES), F32).at[:, :N_EXPERTS].set(moe_w_router[0])
    h, route = _router(x2, row(norm_ffn_pre[1]), shift, scale, wr_pad, seq)
    tm_moe = 512
    tile_expert, n_used, sorted_tok, dest0, dest1 = _routing_plan(route, tm_moe)
    y = _moe_experts(tile_expert, n_used, sorted_tok, h, moe_w_gate[0].astype(BF16), moe_w_up[0].astype(BF16),
                     moe_w_down[0].astype(BF16), tm=tm_moe)
    x2 = _moe_combine(dest0, dest1, y, route, x2, row(norm_ffn_post[1]), gate, seq)
    return x2.reshape(bsz, seq, d)
```

```python
import functools
import math

import jax
import jax.numpy as jnp
from jax import lax
from jax.experimental import pallas as pl
from jax.experimental.pallas import tpu as pltpu

F32 = jnp.float32
BF16 = jnp.bfloat16

LANES = 128
VMEM_LIMIT = 56 * 1024 * 1024
RMS_EPS = 1e-6

S5_GROUP = 16
S5_STATE = 64
CHUNK = 16
OCT = LANES // S5_GROUP
OCT_STATE = 2 * OCT * S5_STATE
N_HEADS = 16
N_EXPERTS = 8
HIGHEST = lax.Precision.HIGHEST


def _cparams(*semantics):
    return pltpu.CompilerParams(dimension_semantics=semantics, vmem_limit_bytes=VMEM_LIMIT)


def _dot(a, b):
    return jnp.dot(a, b, preferred_element_type=F32)


def _rms(xf):
    ms = jnp.mean(xf * xf, axis=-1, keepdims=True)
    return xf * lax.rsqrt(ms + RMS_EPS)


def _norm_mod(xf, g, shift, scale):
    return _rms(xf) * (g * (1.0 + scale)) + shift


def _ada_kernel(c_ref, w_ref, b_ref, o_ref):
    s = c_ref[...]
    s = s * jax.nn.sigmoid(s)
    o_ref[...] = _dot(s.astype(BF16), w_ref[...].astype(BF16)) + b_ref[...]


def _ada(c_pad, w, b, tn=1024):
    nl, d, n = w.shape
    return pl.pallas_call(
        _ada_kernel,
        grid=(nl, n // tn),
        in_specs=[pl.BlockSpec((8, d), lambda l, j: (0, 0)),
                  pl.BlockSpec((None, d, tn), lambda l, j: (l, 0, j)),
                  pl.BlockSpec((None, 1, tn), lambda l, j: (l, 0, j))],
        out_specs=pl.BlockSpec((None, 8, tn), lambda l, j: (l, 0, j)),
        out_shape=jax.ShapeDtypeStruct((nl, 8, n), F32),
        compiler_params=_cparams("parallel", "parallel"),
        name="ada_mods",
    )(c_pad, w, b.reshape(nl, 1, n))


def _expand_kernel(x_ref, rep_ref, o_ref, *, row_shift, col_shift):
    nb, rows, k = x_ref.shape
    n = o_ref.shape[2]
    x = x_ref[...].reshape(nb * rows, k).astype(BF16)
    spread = _dot(x, rep_ref[...])
    group = lax.shift_right_logical(lax.broadcasted_iota(jnp.int32, (nb * rows, n), 0), row_shift) & (OCT - 1)
    slot = lax.shift_right_logical(lax.broadcasted_iota(jnp.int32, (nb * rows, n), 1), col_shift) & (OCT - 1)
    o_ref[...] = jnp.where(group == slot, spread, 0.0).astype(BF16).reshape(nb, rows, n)


def _group_expand(x, rows_per_group, inner, nb):
    b, rows, k = x.shape
    n = k * OCT
    kk = jnp.arange(k, dtype=jnp.int32)
    nn = jnp.arange(n, dtype=jnp.int32)
    rep = jnp.logical_and((kk // inner)[:, None] == (nn // (inner * OCT))[None, :],
                          (kk % inner)[:, None] == (nn % inner)[None, :]).astype(BF16)
    return pl.pallas_call(
        functools.partial(_expand_kernel, row_shift=rows_per_group.bit_length() - 1,
                          col_shift=inner.bit_length() - 1),
        grid=(b // nb,),
        in_specs=[pl.BlockSpec((nb, rows, k), lambda i: (i, 0, 0)),
                  pl.BlockSpec((k, n), lambda i: (0, 0))],
        out_specs=pl.BlockSpec((nb, rows, n), lambda i: (i, 0, 0)),
        out_shape=jax.ShapeDtypeStruct((b, rows, n), BF16),
        compiler_params=_cparams("parallel"),
        name="s5_table_expand",
    )(x, rep)


def _s5_tables(a_re, a_im, b_re, b_im, c_re, c_im, log_step, d_skip, batch):
    g, p = a_re.shape
    h, lc = S5_GROUP, CHUNK
    no = g // OCT
    dt = jnp.exp(log_step)[:, None]
    lr, li = a_re * dt, a_im * dt
    k = jnp.arange(lc + 1, dtype=F32)[:, None, None]
    mag = jnp.exp(k * lr[None])
    pw_r, pw_i = mag * jnp.cos(k * li[None]), mag * jnp.sin(k * li[None])
    nr, ni = pw_r[1] - 1.0, pw_i[1]
    den = a_re * a_re + a_im * a_im
    qr, qi = (nr * a_re + ni * a_im) / den, (ni * a_re - nr * a_im) / den
    bbr = qr[..., None] * b_re - qi[..., None] * b_im
    bbi = qr[..., None] * b_im + qi[..., None] * b_re
    same_group = jnp.eye(OCT, dtype=F32).reshape(1, 1, OCT, 1, 1, OCT, 1)

    wr, wi = pw_r[:lc, :, None, :], pw_i[:lc, :, None, :]
    cwr = c_re[None] * wr - c_im[None] * wi
    cwi = c_re[None] * wi + c_im[None] * wr
    kk = (jnp.einsum('tgop,gpi->tgio', cwr, bbr, precision=HIGHEST)
          - jnp.einsum('tgop,gpi->tgio', cwi, bbi, precision=HIGHEST))
    kk = kk.reshape(lc, no, OCT, h, h).transpose(1, 0, 2, 3, 4)
    kd = (kk[:, :, :, :, None, None, :] * same_group).reshape(no, lc, LANES, LANES)
    kd = jnp.concatenate([jnp.zeros((no, 1, LANES, LANES), F32), kd], axis=1)
    tiles = []
    for delta in range(lc // 2 - 1, -1, -1):
        top = jnp.concatenate([kd[:, 2 * delta + 1], kd[:, 2 * delta + 2]], axis=-1)
        bottom = jnp.concatenate([kd[:, 2 * delta], kd[:, 2 * delta + 1]], axis=-1)
        tiles += [top, bottom]
    m1 = jnp.concatenate(tiles, axis=-2)

    er = c_re[None] * pw_r[1:, :, None, :] - c_im[None] * pw_i[1:, :, None, :]
    ei = c_re[None] * pw_i[1:, :, None, :] + c_im[None] * pw_r[1:, :, None, :]
    e2 = jnp.stack([er, -ei]).reshape(2, lc, no, OCT, h, p).transpose(2, 0, 3, 5, 1, 4)
    m2 = _group_expand(e2.reshape(no * 2, OCT * p, lc * h), p, h, 1).reshape(no, OCT_STATE, lc * LANES)

    fr = pw_r[lc - 1::-1][:lc, :, :, None] * bbr[None] - pw_i[lc - 1::-1][:lc, :, :, None] * bbi[None]
    fi = pw_r[lc - 1::-1][:lc, :, :, None] * bbi[None] + pw_i[lc - 1::-1][:lc, :, :, None] * bbr[None]
    f2 = jnp.stack([fr, fi]).reshape(2, lc, no, OCT, p, h).transpose(2, 1, 3, 5, 0, 4)
    m3 = _group_expand(f2.reshape(no * lc, OCT * h, 2 * p), h, p, 8).reshape(no, lc * LANES, OCT_STATE)

    ar = pw_r[lc].reshape(no, OCT * p)
    ai = pw_i[lc].reshape(no, OCT * p)
    a1 = jnp.tile(jnp.concatenate([ar, ar], axis=-1), (1, batch))[:, None, :]
    a2 = jnp.tile(jnp.concatenate([-ai, ai], axis=-1), (1, batch))[:, None, :]
    dsk = jnp.tile(d_skip.reshape(no, 1, LANES), (1, 1, lc))
    return m1.astype(BF16), m2.astype(BF16), m3.astype(BF16), a1, a2, dsk


def _s5_pre_kernel(x_ref, g_ref, sh_ref, sc_ref, o_ref, slab):
    tm = x_ref.shape[0]
    nt = x_ref.shape[1] // LANES
    rows = tm // CHUNK
    h = _norm_mod(x_ref[...], g_ref[...], sh_ref[...], sc_ref[...])
    for j in range(nt):
        slab[j] = h[:, j * LANES:(j + 1) * LANES]
    for j in range(nt):
        for t in range(CHUNK):
            o_ref[j, :, t * LANES:(t + 1) * LANES] = slab.at[j][pl.ds(t, rows, stride=CHUNK), :].astype(BF16)


def _s5_pre(x2, g, shift, scale, seq, tm=512):
    t, d = x2.shape
    nt = d // LANES
    per_b = seq // tm
    return pl.pallas_call(
        _s5_pre_kernel,
        grid=(t // tm,),
        in_specs=[pl.BlockSpec((tm, d), lambda i: (i, 0)),
                  pl.BlockSpec((1, d), lambda i: (0, 0)),
                  pl.BlockSpec((None, 1, d), lambda i: (i // per_b, 0, 0)),
                  pl.BlockSpec((None, 1, d), lambda i: (i // per_b, 0, 0))],
        out_specs=pl.BlockSpec((nt, tm // CHUNK, CHUNK * LANES), lambda i: (0, i, 0)),
        out_shape=jax.ShapeDtypeStruct((nt, t // CHUNK, CHUNK * LANES), BF16),
        scratch_shapes=[pltpu.VMEM((nt, tm, LANES), F32)],
        compiler_params=_cparams("parallel"),
        name="s5_pre",
    )(x2, g, shift, scale)


def _s5_state_kernel(h_ref, m3_ref, o_ref):
    o_ref[...] = _dot(h_ref[...], m3_ref[...])


def _s5_state(ho, m3, batch):
    no, r, kc = ho.shape
    c = r // batch
    return pl.pallas_call(
        _s5_state_kernel,
        grid=(no, batch),
        in_specs=[pl.BlockSpec((None, c, kc), lambda j, b: (j, b, 0)),
                  pl.BlockSpec((None, kc, OCT_STATE), lambda j, b: (j, 0, 0))],
        out_specs=pl.BlockSpec((None, c, OCT_STATE), lambda j, b: (j, 0, b)),
        out_shape=jax.ShapeDtypeStruct((no, c, batch * OCT_STATE), F32),
        compiler_params=_cparams("parallel", "parallel"),
        name="s5_state",
    )(ho, m3)


def _s5_scan_kernel(v_ref, a1_ref, a2_ref, o_ref, *, batch, unroll):
    a1 = a1_ref[...]
    a2 = a2_ref[...]
    half = OCT_STATE // 2

    def swap(s):
        parts = []
        for b in range(batch):
            parts.append(s[:, b * OCT_STATE + half:(b + 1) * OCT_STATE])
            parts.append(s[:, b * OCT_STATE:b * OCT_STATE + half])
        return jnp.concatenate(parts, axis=-1)

    def body(i, s):
        base = pl.multiple_of(i * unroll, unroll)
        v = v_ref[pl.ds(base, unroll), :]
        rows = []
        for k in range(unroll):
            rows.append(s)
            s = a1 * s + a2 * swap(s) + v[k:k + 1, :]
        o_ref[pl.ds(base, unroll), :] = jnp.concatenate(rows, axis=0).astype(o_ref.dtype)
        return s

    lax.fori_loop(0, v_ref.shape[0] // unroll, body, jnp.zeros((1, v_ref.shape[1]), F32))


def _s5_scan(v, a1, a2, batch, unroll=16):
    no, c, w = v.shape
    return pl.pallas_call(
        functools.partial(_s5_scan_kernel, batch=batch, unroll=unroll),
        grid=(no,),
        in_specs=[pl.BlockSpec((None, c, w), lambda j: (j, 0, 0)),
                  pl.BlockSpec((None, 1, w), lambda j: (j, 0, 0)),
                  pl.BlockSpec((None, 1, w), lambda j: (j, 0, 0))],
        out_specs=pl.BlockSpec((None, c, w), lambda j: (j, 0, 0)),
        out_shape=jax.ShapeDtypeStruct((no, c, w), BF16),
        compiler_params=_cparams("parallel"),
        name="s5_scan",
    )(v, a1, a2)


def _s5_out_kernel(h_ref, s_ref, m1_ref, m2_ref, d_ref, o_ref):
    tw = m1_ref.shape[1]
    n_tiles = h_ref.shape[1] // tw
    for tile in range(n_tiles):
        cols = slice(tile * tw, (tile + 1) * tw)
        y = (_dot(h_ref[:, :(tile + 1) * tw], m1_ref[(n_tiles - 1 - tile) * tw:, :])
             + _dot(s_ref[...], m2_ref[:, cols])
             + d_ref[:, cols] * h_ref[:, cols].astype(F32))
        o_ref[:, cols] = jax.nn.gelu(y).astype(BF16)


def _s5_out(ho, sin, m1, m2, dsk, batch):
    no, r, kc = ho.shape
    c = r // batch
    return pl.pallas_call(
        _s5_out_kernel,
        grid=(no, batch),
        in_specs=[pl.BlockSpec((None, c, kc), lambda j, b: (j, b, 0)),
                  pl.BlockSpec((None, c, OCT_STATE), lambda j, b: (j, 0, b)),
                  pl.BlockSpec((None, kc, m1.shape[2]), lambda j, b: (j, 0, 0)),
                  pl.BlockSpec((None, OCT_STATE, kc), lambda j, b: (j, 0, 0)),
                  pl.BlockSpec((None, 1, kc), lambda j, b: (j, 0, 0))],
        out_specs=pl.BlockSpec((None, c, kc), lambda j, b: (j, b, 0)),
        out_shape=jax.ShapeDtypeStruct((no, r, kc), BF16),
        compiler_params=_cparams("parallel", "parallel"),
        name="s5_out",
    )(ho, sin, m1, m2, dsk)


def _s5_glu_kernel(y_ref, x_ref, w_ref, b_ref, g_ref, gate_ref, o_ref, ybuf, slab):
    nt, rows, _ = y_ref.shape
    for t in range(CHUNK):
        for j in range(nt):
            ybuf[t * rows:(t + 1) * rows, j * LANES:(j + 1) * LANES] = y_ref[j, :, t * LANES:(t + 1) * LANES]
    y = ybuf[...]
    z = _dot(y, w_ref[...]) + b_ref[...]
    m = y.astype(F32) * jax.nn.sigmoid(z)
    res = (1.0 + gate_ref[...]) * (_rms(m) * g_ref[...])
    for j in range(nt):
        for t in range(CHUNK):
            slab.at[j][pl.ds(t, rows, stride=CHUNK), :] = res[t * rows:(t + 1) * rows, j * LANES:(j + 1) * LANES]
    for j in range(nt):
        o_ref[:, j * LANES:(j + 1) * LANES] = x_ref[:, j * LANES:(j + 1) * LANES] + slab[j]


def _s5_glu(yo, x2, w, b, g, gate, seq, tm=512):
    t, d = x2.shape
    nt = d // LANES
    per_b = seq // tm
    return pl.pallas_call(
        _s5_glu_kernel,
        grid=(t // tm,),
        in_specs=[pl.BlockSpec((nt, tm // CHUNK, CHUNK * LANES), lambda i: (0, i, 0)),
                  pl.BlockSpec((tm, d), lambda i: (i, 0)),
                  pl.BlockSpec((d, d), lambda i: (0, 0)),
                  pl.BlockSpec((1, d), lambda i: (0, 0)),
                  pl.BlockSpec((1, d), lambda i: (0, 0)),
                  pl.BlockSpec((None, 1, d), lambda i: (i // per_b, 0, 0))],
        out_specs=pl.BlockSpec((tm, d), lambda i: (i, 0)),
        out_shape=jax.ShapeDtypeStruct((t, d), F32),
        scratch_shapes=[pltpu.VMEM((tm, d), BF16), pltpu.VMEM((nt, tm, LANES), F32)],
        compiler_params=_cparams("parallel"),
        name="s5_glu",
    )(yo, x2, w, b, g, gate)


FFN_TF = 1024


def _swiglu_step(h, wg_ref, wu_ref, wd_ref, hid_scr):
    for c in range(hid_scr.shape[1] // LANES):
        cols = slice(c * LANES, (c + 1) * LANES)
        r = _dot(h, jnp.concatenate([wg_ref[:, cols], wu_ref[:, cols]], axis=1))
        a, u = r[:, :LANES], r[:, LANES:]
        hid_scr[:, cols] = (a * jax.nn.sigmoid(a) * u).astype(BF16)
    return _dot(hid_scr[...], wd_ref[...])


def _ffn_kernel(x_ref, g_ref, sh_ref, sc_ref, wg_ref, wu_ref, wd_ref, gp_ref, gate_ref, o_ref, h_scr, hid_scr):
    j = pl.program_id(1)

    @pl.when(j == 0)
    def _():
        h_scr[...] = _norm_mod(x_ref[...], g_ref[...], sh_ref[...], sc_ref[...]).astype(BF16)
        o_ref[...] = jnp.zeros_like(o_ref)

    o_ref[...] += _swiglu_step(h_scr[...], wg_ref, wu_ref, wd_ref, hid_scr)

    @pl.when(j == pl.num_programs(1) - 1)
    def _():
        o_ref[...] = x_ref[...] + (1.0 + gate_ref[...]) * (_rms(o_ref[...]) * gp_ref[...])


def _ffn(x2, g, shift, scale, wg, wu, wd, gp, gate, seq, tm=512):
    t, d = x2.shape
    f = wd.shape[0]
    per_b = seq // tm
    mod = pl.BlockSpec((None, 1, d), lambda i, j: (i // per_b, 0, 0))
    row = pl.BlockSpec((1, d), lambda i, j: (0, 0))
    wcol = pl.BlockSpec((d, FFN_TF), lambda i, j: (0, j))
    return pl.pallas_call(
        _ffn_kernel,
        grid=(t // tm, f // FFN_TF),
        in_specs=[pl.BlockSpec((tm, d), lambda i, j: (i, 0)), row, mod, mod, wcol, wcol,
                  pl.BlockSpec((FFN_TF, d), lambda i, j: (j, 0)),
                  row, mod],
        out_specs=pl.BlockSpec((tm, d), lambda i, j: (i, 0)),
        out_shape=jax.ShapeDtypeStruct((t, d), F32),
        scratch_shapes=[pltpu.VMEM((tm, d), BF16), pltpu.VMEM((tm, FFN_TF), BF16)],
        compiler_params=_cparams("parallel", "arbitrary"),
        name="dense_ffn",
    )(x2, g, shift, scale, wg, wu, wd, gp, gate)


def _qkv_kernel(x_ref, gq_ref, shq_ref, scq_ref, gkv_ref, shkv_ref, sckv_ref, w_ref, o_ref, hq, hkv, *, qscale):
    n = pl.program_id(1)

    @pl.when(n == 0)
    def _():
        xf = x_ref[...]
        xn = _rms(xf)
        hq[...] = (xn * (gq_ref[...] * (1.0 + scq_ref[...])) + shq_ref[...]).astype(BF16)
        hkv[...] = (xn * (gkv_ref[...] * (1.0 + sckv_ref[...])) + shkv_ref[...]).astype(BF16)
        o_ref[...] = (_dot(hq[...], w_ref[...]) * qscale).astype(BF16)

    @pl.when(n > 0)
    def _():
        o_ref[...] = _dot(hkv[...], w_ref[...]).astype(BF16)


def _qkv(x2, gq, shq, scq, gkv, shkv, sckv, w3, seq, tm=512):
    t, d = x2.shape
    per_b = seq // tm
    mod = pl.BlockSpec((None, 1, d), lambda i, n: (i // per_b, 0, 0))
    row = pl.BlockSpec((1, d), lambda i, n: (0, 0))
    qscale = math.log2(math.e) / math.sqrt(d // N_HEADS)
    return pl.pallas_call(
        functools.partial(_qkv_kernel, qscale=qscale),
        grid=(t // tm, 3),
        in_specs=[pl.BlockSpec((tm, d), lambda i, n: (i, 0)), row, mod, mod, row, mod, mod,
                  pl.BlockSpec((None, d, d), lambda i, n: (n, 0, 0))],
        out_specs=pl.BlockSpec((None, tm, d), lambda i, n: (n, i, 0)),
        out_shape=jax.ShapeDtypeStruct((3, t, d), BF16),
        scratch_shapes=[pltpu.VMEM((tm, d), BF16), pltpu.VMEM((tm, d), BF16)],
        compiler_params=_cparams("parallel", "arbitrary"),
        name="qkv_proj",
    )(x2, gq, shq, scq, gkv, shkv, sckv, w3)


ATTN_TK = 256
EXP2_ZERO_BELOW = -150.0


def _attn_kernel(q_ref, k_ref, v_ref, o_ref, *, tb, tk, nh, hd):
    qi = pl.program_id(2)
    q_pos = qi * tb + lax.broadcasted_iota(jnp.int32, (tb, tk), 0)
    k_off = lax.broadcasted_iota(jnp.int32, (tb, tk), 1)
    krow = lax.broadcasted_iota(jnp.int32, (tk, tk), 0)
    kcol = lax.broadcasted_iota(jnp.int32, (tk, tk), 1)
    later = jnp.where(krow > kcol, 1.0, 0.0).astype(BF16)
    later2 = jnp.concatenate([later, later], axis=0)
    sign_bit = jnp.uint32(0x80000000)

    heads = range(nh)
    lanes = [slice(g * hd, (g + 1) * hd) for g in heads]

    def walk(spans, accs, runs):
        pairs = [(b, g) for b in range(len(spans)) for g in heads]
        start = [pl.multiple_of(s, tb) for s, _ in spans]
        keep = [(s + k_off) < lim for s, lim in spans]
        y, s, sm, suffix = {}, {}, {}, {}
        for b, g in pairs:
            y[b, g] = lax.dot_general(q_ref[:, lanes[g]], k_ref[pl.ds(start[b], tk), lanes[g]],
                                      (((1,), (1,)), ((), ())), preferred_element_type=F32)
        for b, g in pairs:
            neg_abs = lax.bitcast_convert_type(lax.bitcast_convert_type(y[b, g], jnp.uint32) | sign_bit, F32)
            s[b, g] = jnp.maximum(y[b, g], 0.0) + jnp.log2(1.0 + jnp.exp2(neg_abs))
            sm[b, g] = jnp.where(keep[b], s[b, g], 0.0)
        for b, g in pairs:
            hi = sm[b, g].astype(BF16)
            lo = (sm[b, g] - hi.astype(F32)).astype(BF16)
            suffix[b, g] = _dot(jnp.concatenate([hi, lo], axis=1), later2)
        new_accs, new_runs = [], []
        for g in heads:
            run, acc = runs[g], accs[g]
            for b in range(len(spans)):
                w = jnp.where(keep[b], jnp.exp2(((y[b, g] - s[b, g]) - suffix[b, g]) - run), 0.0)
                acc = acc + _dot(w.astype(BF16), v_ref[pl.ds(start[b], tk), lanes[g]])
                run = run + suffix[b, g][:, 0:1] + sm[b, g][:, 0:1]
            new_accs.append(acc)
            new_runs.append(run)
        return tuple(new_accs), tuple(new_runs)

    accs = tuple(jnp.zeros((tb, hd), F32) for _ in heads)
    runs = tuple(jnp.zeros((tb, 1), F32) for _ in heads)
    start_a = jnp.maximum((qi + 1) * tb - tk, 0)
    start_b = jnp.maximum(start_a - tk, 0)
    accs, runs = walk([(start_a, q_pos), (start_b, start_a)], accs, runs)

    def cond(c):
        left, _, runs = c
        low = runs[0]
        for g in range(1, nh):
            low = jnp.minimum(low, runs[g])
        return jnp.logical_and(left > 0, jnp.min(low) <= -EXP2_ZERO_BELOW)

    def body(c):
        left, accs, runs = c
        start = jnp.maximum(left - tk, 0)
        accs, runs = walk([(start, left)], accs, runs)
        return start, accs, runs

    _, accs, _ = lax.while_loop(cond, body, (start_b, accs, runs))
    for g in range(nh):
        o_ref[:, g * hd:(g + 1) * hd] = accs[g].astype(o_ref.dtype)


def _attention(qkv, batch, seq, tb=256, tk=ATTN_TK, nh=4):
    _, t, d = qkv.shape
    hd = d // N_HEADS
    nq = seq // tb
    return pl.pallas_call(
        functools.partial(_attn_kernel, tb=tb, tk=tk, nh=nh, hd=hd),
        grid=(batch, N_HEADS // nh, nq),
        in_specs=[pl.BlockSpec((None, tb, nh * hd), lambda b, h, i: (0, b * nq + i, h)),
                  pl.BlockSpec((None, seq, nh * hd), lambda b, h, i: (1, b, h)),
                  pl.BlockSpec((None, seq, nh * hd), lambda b, h, i: (2, b, h))],
        out_specs=pl.BlockSpec((tb, nh * hd), lambda b, h, i: (b * nq + i, h)),
        out_shape=jax.ShapeDtypeStruct((t, d), BF16),
        compiler_params=_cparams("parallel", "parallel", "parallel"),
        name="stickbreak_attn",
    )(qkv, qkv, qkv)


def _oproj_kernel(a_ref, x_ref, w_ref, g_ref, gate_ref, o_ref):
    m = _dot(a_ref[...], w_ref[...])
    o_ref[...] = x_ref[...] + (1.0 + gate_ref[...]) * (_rms(m) * g_ref[...])


def _oproj(a, x2, w, g, gate, seq, tm=512):
    t, d = x2.shape
    per_b = seq // tm
    return pl.pallas_call(
        _oproj_kernel,
        grid=(t // tm,),
        in_specs=[pl.BlockSpec((tm, d), lambda i: (i, 0)),
                  pl.BlockSpec((tm, d), lambda i: (i, 0)),
                  pl.BlockSpec((d, d), lambda i: (0, 0)),
                  pl.BlockSpec((1, d), lambda i: (0, 0)),
                  pl.BlockSpec((None, 1, d), lambda i: (i // per_b, 0, 0))],
        out_specs=pl.BlockSpec((tm, d), lambda i: (i, 0)),
        out_shape=jax.ShapeDtypeStruct((t, d), F32),
        compiler_params=_cparams("parallel"),
        name="attn_out_proj",
    )(a, x2, w, g, gate)


def _router_kernel(x_ref, g_ref, sh_ref, sc_ref, wr_ref, h_ref, r_ref):
    h = _norm_mod(x_ref[...], g_ref[...], sh_ref[...], sc_ref[...])
    h_ref[...] = h
    logits = jnp.dot(h, wr_ref[...], preferred_element_type=F32, precision=HIGHEST)
    lane = lax.broadcasted_iota(jnp.int32, logits.shape, 1).astype(F32)
    neg = jnp.float32(-jnp.inf)
    lg = jnp.where(lane < N_EXPERTS, logits, neg)
    m1 = jnp.max(lg, axis=-1, keepdims=True)
    i1 = jnp.min(jnp.where(lg == m1, lane, float(LANES)), axis=-1, keepdims=True)
    lg2 = jnp.where(lane == i1, neg, lg)
    m2 = jnp.max(lg2, axis=-1, keepdims=True)
    i2 = jnp.min(jnp.where(lg2 == m2, lane, float(LANES)), axis=-1, keepdims=True)
    e = jnp.exp(m2 - m1)
    p1 = 1.0 / (1.0 + e)
    p2 = e * p1
    r_ref[...] = jnp.where(lane == 0, i1, jnp.where(lane == 1, i2, jnp.where(lane == 2, p1, jnp.where(lane == 3, p2, 0.0))))


def _router(x2, g, shift, scale, wr_pad, seq, tm=512):
    t, d = x2.shape
    per_b = seq // tm
    mod = pl.BlockSpec((None, 1, d), lambda i: (i // per_b, 0, 0))
    return pl.pallas_call(
        _router_kernel,
        grid=(t // tm,),
        in_specs=[pl.BlockSpec((tm, d), lambda i: (i, 0)),
                  pl.BlockSpec((1, d), lambda i: (0, 0)), mod, mod,
                  pl.BlockSpec((d, LANES), lambda i: (0, 0))],
        out_specs=[pl.BlockSpec((tm, d), lambda i: (i, 0)),
                   pl.BlockSpec((tm, LANES), lambda i: (i, 0))],
        out_shape=[jax.ShapeDtypeStruct((t, d), F32), jax.ShapeDtypeStruct((t, LANES), F32)],
        compiler_params=_cparams("parallel"),
        name="moe_router",
    )(x2, g, shift, scale, wr_pad)


def _row_copy(src_hbm, dst, sem, src_row, dst_row):
    return pltpu.make_async_copy(src_hbm.at[pl.ds(src_row, 1)], dst.at[pl.ds(dst_row, 1)], sem)


def _moe_kernel(te_ref, nu_ref, tok_ref, nxt_ref, h_hbm, wg_ref, wu_ref, wd_ref, o_ref, xbuf, xb16, hid_scr, sem):
    i = pl.program_id(0)
    j = pl.program_id(1)
    tm = xb16.shape[0]
    used = i < nu_ref[0]
    slot = lax.rem(i, 2)

    def gather(idx_ref, s):
        def start(r, c):
            _row_copy(h_hbm, xbuf.at[s], sem.at[s], idx_ref[0, r], r).start()
            return c
        lax.fori_loop(0, tm, start, 0, unroll=8)

    @pl.when(jnp.logical_and(i == 0, j == 0))
    def _():
        gather(tok_ref, 0)

    @pl.when(jnp.logical_and(i + 1 < nu_ref[0], j == 0))
    def _():
        gather(nxt_ref, 1 - slot)

    @pl.when(jnp.logical_and(used, j == 0))
    def _():
        def wait(r, c):
            _row_copy(h_hbm, xbuf.at[slot], sem.at[slot], 0, r).wait()
            return c
        lax.fori_loop(0, tm, wait, 0, unroll=8)
        xb16[...] = xbuf[slot].astype(BF16)

    @pl.when(j == 0)
    def _():
        o_ref[...] = jnp.zeros_like(o_ref)

    @pl.when(used)
    def _():
        o_ref[...] += _swiglu_step(xb16[...], wg_ref, wu_ref, wd_ref, hid_scr)


def _moe_experts(tile_expert, n_used, sorted_tok, h, wg, wu, wd, tm=512):
    t, d = h.shape
    f = wd.shape[1]
    nt = sorted_tok.shape[0]
    nf = f // FFN_TF

    def wcol(i, j, te, nu):
        return (te[i], 0, jnp.where(i < nu[0], j, nf - 1))

    def wrow(i, j, te, nu):
        return (te[i], jnp.where(i < nu[0], j, nf - 1), 0)

    grid_spec = pltpu.PrefetchScalarGridSpec(
        num_scalar_prefetch=2,
        grid=(nt, nf),
        in_specs=[pl.BlockSpec((None, 1, tm), lambda i, j, te, nu: (i, 0, 0), memory_space=pltpu.SMEM),
                  pl.BlockSpec((None, 1, tm), lambda i, j, te, nu: (jnp.minimum(i + 1, nt - 1), 0, 0),
                               memory_space=pltpu.SMEM),
                  pl.BlockSpec(memory_space=pl.ANY),
                  pl.BlockSpec((None, d, FFN_TF), wcol),
                  pl.BlockSpec((None, d, FFN_TF), wcol),
                  pl.BlockSpec((None, FFN_TF, d), wrow)],
        out_specs=pl.BlockSpec((tm, d), lambda i, j, te, nu: (i, 0)),
        scratch_shapes=[pltpu.VMEM((2, tm, d), F32), pltpu.VMEM((tm, d), BF16), pltpu.VMEM((tm, FFN_TF), BF16),
                        pltpu.SemaphoreType.DMA((2,))],
    )
    return pl.pallas_call(
        _moe_kernel,
        grid_spec=grid_spec,
        out_shape=jax.ShapeDtypeStruct((nt * tm, d), F32),
        compiler_params=_cparams("arbitrary", "arbitrary"),
        name="moe_experts",
    )(tile_expert, n_used, sorted_tok, sorted_tok, h, wg, wu, wd)


def _combine_kernel(d0_ref, d1_ref, n0_ref, n1_ref, y_hbm, r_ref, x_ref, g_ref, gate_ref, o_ref, buf, sem):
    i = pl.program_id(0)
    tm = x_ref.shape[0]
    slot = lax.rem(i, 2)

    def gather(i0_ref, i1_ref, s):
        def start(r, c):
            _row_copy(y_hbm, buf.at[s, 0], sem.at[s], i0_ref[0, r], r).start()
            _row_copy(y_hbm, buf.at[s, 1], sem.at[s], i1_ref[0, r], r).start()
            return c
        lax.fori_loop(0, tm, start, 0, unroll=8)

    @pl.when(i == 0)
    def _():
        gather(d0_ref, d1_ref, 0)

    @pl.when(i + 1 < pl.num_programs(0))
    def _():
        gather(n0_ref, n1_ref, 1 - slot)

    def wait(r, c):
        _row_copy(y_hbm, buf.at[slot, 0], sem.at[slot], 0, r).wait()
        _row_copy(y_hbm, buf.at[slot, 1], sem.at[slot], 0, r).wait()
        return c

    lax.fori_loop(0, tm, wait, 0, unroll=8)
    r = r_ref[...]
    f = r[:, 2:3] * buf[slot, 0] + r[:, 3:4] * buf[slot, 1]
    o_ref[...] = x_ref[...] + (1.0 + gate_ref[...]) * (_rms(f) * g_ref[...])


def _moe_combine(dest0, dest1, y, route, x2, g, gate, seq, tm=256):
    t, d = x2.shape
    per_b = seq // tm
    nt = t // tm
    idx = pl.BlockSpec((None, 1, tm), lambda i: (i, 0, 0), memory_space=pltpu.SMEM)
    nxt = pl.BlockSpec((None, 1, tm), lambda i: (jnp.minimum(i + 1, nt - 1), 0, 0), memory_space=pltpu.SMEM)
    d0 = dest0.reshape(nt, 1, tm)
    d1 = dest1.reshape(nt, 1, tm)
    return pl.pallas_call(
        _combine_kernel,
        grid=(nt,),
        in_specs=[idx, idx, nxt, nxt,
                  pl.BlockSpec(memory_space=pl.ANY),
                  pl.BlockSpec((tm, LANES), lambda i: (i, 0)),
                  pl.BlockSpec((tm, d), lambda i: (i, 0)),
                  pl.BlockSpec((1, d), lambda i: (0, 0)),
                  pl.BlockSpec((None, 1, d), lambda i: (i // per_b, 0, 0))],
        out_specs=pl.BlockSpec((tm, d), lambda i: (i, 0)),
        out_shape=jax.ShapeDtypeStruct((t, d), F32),
        scratch_shapes=[pltpu.VMEM((2, 2, tm, d), F32), pltpu.SemaphoreType.DMA((2,))],
        compiler_params=_cparams("arbitrary"),
        name="moe_combine",
    )(d0, d1, d0, d1, y, route, x2, g, gate)


def _routing_plan(route, tm):
    t = route.shape[0]
    ex = route[:, 0:2].astype(jnp.int32).T.reshape(-1)
    onehot = (ex[:, None] == jnp.arange(N_EXPERTS, dtype=jnp.int32)[None, :]).astype(jnp.int32)
    csum = jnp.cumsum(onehot, axis=0)
    rank = jnp.sum(onehot * csum, axis=1) - 1
    counts = csum[-1]
    tiles = (counts + tm - 1) // tm
    tile_end = jnp.cumsum(tiles)
    offset = (tile_end - tiles) * tm
    dest = offset[ex] + rank
    nt = (2 * t) // tm + N_EXPERTS
    n_used = tile_end[-1]
    tile_ids = jnp.minimum(jnp.arange(nt, dtype=jnp.int32), n_used - 1)
    tile_expert = jnp.sum((tile_ids[:, None] >= tile_end[None, :]).astype(jnp.int32), axis=1)
    token = jnp.arange(2 * t, dtype=jnp.int32) % t
    sorted_tok = jnp.zeros((nt * tm,), jnp.int32).at[dest].set(token)
    return (tile_expert.astype(jnp.int32), n_used.reshape(1).astype(jnp.int32),
            sorted_tok.reshape(nt, 1, tm), dest[:t], dest[t:])


def kernel(x, c, ada_mix_w, ada_mix_b, norm_mix_pre, norm_mix_post, ada_ffn_w, ada_ffn_b, norm_ffn_pre, norm_ffn_post, s5_a_re, s5_a_im, s5_b_re, s5_b_im, s5_c_re, s5_c_im, s5_d, s5_log_step, s5_glu_w, s5_glu_b, kv_ada_w, kv_ada_b, kv_norm, w_k, w_v, w_q, w_o, ffn_w_gate, ffn_w_up, ffn_w_down, moe_w_router, moe_w_gate, moe_w_up, moe_w_down):
    bsz, seq, d = x.shape
    x2 = x.reshape(bsz * seq, d)

    c_pad = jnp.zeros((8, d), F32).at[:bsz].set(c)
    mix = _ada(c_pad, ada_mix_w, ada_mix_b)
    ffn = _ada(c_pad, ada_ffn_w, ada_ffn_b)
    kvm = _ada(c_pad, kv_ada_w[None], kv_ada_b[None])

    def mods(m, layer, n):
        return [m[layer, :bsz, k * d:(k + 1) * d].reshape(bsz, 1, d) for k in range(n)]

    def row(v):
        return v.reshape(1, d)

    shift, scale, gate = mods(mix, 0, 3)
    m1, m2, m3, a1, a2, dsk = _s5_tables(s5_a_re[0], s5_a_im[0], s5_b_re[0], s5_b_im[0],
                                         s5_c_re[0], s5_c_im[0], s5_log_step[0], s5_d[0], bsz)
    ho = _s5_pre(x2, row(norm_mix_pre[0]), shift, scale, seq)
    v = _s5_state(ho, m3, bsz)
    sin = _s5_scan(v, a1, a2, bsz)
    yo = _s5_out(ho, sin, m1, m2, dsk, bsz)
    x2 = _s5_glu(yo, x2, s5_glu_w[0].astype(BF16), row(s5_glu_b[0]), row(norm_mix_post[0]), gate, seq)

    shift, scale, gate = mods(ffn, 0, 3)
    x2 = _ffn(x2, row(norm_ffn_pre[0]), shift, scale, ffn_w_gate[0].astype(BF16), ffn_w_up[0].astype(BF16),
              ffn_w_down[0].astype(BF16), row(norm_ffn_post[0]), gate, seq)

    shift, scale, gate = mods(mix, 1, 3)
    kv_shift, kv_scale = mods(kvm, 0, 2)
    w3 = jnp.stack([w_q[0], w_k, w_v]).astype(BF16)
    qkv = _qkv(x2, row(norm_mix_pre[1]), shift, scale, row(kv_norm), kv_shift, kv_scale, w3, seq)
    att = _attention(qkv, bsz, seq)
    x2 = _oproj(att, x2, w_o[0].astype(BF16), row(norm_mix_post[1]), gate, seq)

    shift, scale, gate = mods(ffn, 1, 3)
    wr_pad = jnp.zeros((d, LANES), F32).at[:, :N_EXPERTS].set(moe_w_router[0])
    h, route = _router(x2, row(norm_ffn_pre[1]), shift, scale, wr_pad, seq)
    tm_moe = 512
    tile_expert, n_used, sorted_tok, dest0, dest1 = _routing_plan(route, tm_moe)
    y = _moe_experts(tile_expert, n_used, sorted_tok, h, moe_w_gate[0].astype(BF16), moe_w_up[0].astype(BF16),
                     moe_w_down[0].astype(BF16), tm=tm_moe)
    x2 = _moe_combine(dest0, dest1, y, route, x2, row(norm_ffn_post[1]), gate, seq)
    return x2.reshape(bsz, seq, d)
```

```python
import functools
import math

import jax
import jax.numpy as jnp
from jax import lax
from jax.experimental import pallas as pl
from jax.experimental.pallas import tpu as pltpu

F32 = jnp.float32
BF16 = jnp.bfloat16

LANES = 128
VMEM_LIMIT = 56 * 1024 * 1024
RMS_EPS = 1e-6

S5_GROUP = 16
S5_STATE = 64
CHUNK = 16
OCT = LANES // S5_GROUP
OCT_STATE = 2 * OCT * S5_STATE
N_HEADS = 16
N_EXPERTS = 8
HIGHEST = lax.Precision.HIGHEST


def _cparams(*semantics):
    return pltpu.CompilerParams(dimension_semantics=semantics, vmem_limit_bytes=VMEM_LIMIT)


def _dot(a, b):
    return jnp.dot(a, b, preferred_element_type=F32)


def _rms(xf):
    ms = jnp.mean(xf * xf, axis=-1, keepdims=True)
    return xf * lax.rsqrt(ms + RMS_EPS)


def _norm_mod(xf, g, shift, scale):
    return _rms(xf) * (g * (1.0 + scale)) + shift


def _ada_kernel(c_ref, w_ref, b_ref, o_ref):
    s = c_ref[...]
    s = s * jax.nn.sigmoid(s)
    o_ref[...] = _dot(s.astype(BF16), w_ref[...].astype(BF16)) + b_ref[...]


def _ada(c_pad, w, b, tn=1024):
    nl, d, n = w.shape
    return pl.pallas_call(
        _ada_kernel,
        grid=(nl, n // tn),
        in_specs=[pl.BlockSpec((8, d), lambda l, j: (0, 0)),
                  pl.BlockSpec((None, d, tn), lambda l, j: (l, 0, j)),
                  pl.BlockSpec((None, 1, tn), lambda l, j: (l, 0, j))],
        out_specs=pl.BlockSpec((None, 8, tn), lambda l, j: (l, 0, j)),
        out_shape=jax.ShapeDtypeStruct((nl, 8, n), F32),
        compiler_params=_cparams("parallel", "parallel"),
        name="ada_mods",
    )(c_pad, w, b.reshape(nl, 1, n))


def _expand_kernel(x_ref, rep_ref, o_ref, *, row_shift, col_shift):
    nb, rows, k = x_ref.shape
    n = o_ref.shape[2]
    x = x_ref[...].reshape(nb * rows, k).astype(BF16)
    spread = _dot(x, rep_ref[...])
    group = lax.shift_right_logical(lax.broadcasted_iota(jnp.int32, (nb * rows, n), 0), row_shift) & (OCT - 1)
    slot = lax.shift_right_logical(lax.broadcasted_iota(jnp.int32, (nb * rows, n), 1), col_shift) & (OCT - 1)
    o_ref[...] = jnp.where(group == slot, spread, 0.0).astype(BF16).reshape(nb, rows, n)


def _group_expand(x, rows_per_group, inner, nb):
    b, rows, k = x.shape
    n = k * OCT
    kk = jnp.arange(k, dtype=jnp.int32)
    nn = jnp.arange(n, dtype=jnp.int32)
    rep = jnp.logical_and((kk // inner)[:, None] == (nn // (inner * OCT))[None, :],
                          (kk % inner)[:, None] == (nn % inner)[None, :]).astype(BF16)
    return pl.pallas_call(
        functools.partial(_expand_kernel, row_shift=rows_per_group.bit_length() - 1,
                          col_shift=inner.bit_length() - 1),
        grid=(b // nb,),
        in_specs=[pl.BlockSpec((nb, rows, k), lambda i: (i, 0, 0)),
                  pl.BlockSpec((k, n), lambda i: (0, 0))],
        out_specs=pl.BlockSpec((nb, rows, n), lambda i: (i, 0, 0)),
        out_shape=jax.ShapeDtypeStruct((b, rows, n), BF16),
        compiler_params=_cparams("parallel"),
        name="s5_table_expand",
    )(x, rep)


def _s5_tables(a_re, a_im, b_re, b_im, c_re, c_im, log_step, d_skip, batch):
    g, p = a_re.shape
    h, lc = S5_GROUP, CHUNK
    no = g // OCT
    dt = jnp.exp(log_step)[:, None]
    lr, li = a_re * dt, a_im * dt
    k = jnp.arange(lc + 1, dtype=F32)[:, None, None]
    mag = jnp.exp(k * lr[None])
    pw_r, pw_i = mag * jnp.cos(k * li[None]), mag * jnp.sin(k * li[None])
    nr, ni = pw_r[1] - 1.0, pw_i[1]
    den = a_re * a_re + a_im * a_im
    qr, qi = (nr * a_re + ni * a_im) / den, (ni * a_re - nr * a_im) / den
    bbr = qr[..., None] * b_re - qi[..., None] * b_im
    bbi = qr[..., None] * b_im + qi[..., None] * b_re
    same_group = jnp.eye(OCT, dtype=F32).reshape(1, 1, OCT, 1, 1, OCT, 1)

    wr, wi = pw_r[:lc, :, None, :], pw_i[:lc, :, None, :]
    cwr = c_re[None] * wr - c_im[None] * wi
    cwi = c_re[None] * wi + c_im[None] * wr
    kk = jnp.einsum('tgop,gpi->tgio', jnp.concatenate([cwr, -cwi], axis=-1),
                    jnp.concatenate([bbr, bbi], axis=1), precision=HIGHEST)
    kk = kk.reshape(lc, no, OCT, h, h).transpose(1, 0, 2, 3, 4)
    kd = (kk[:, :, :, :, None, None, :] * same_group).reshape(no, lc, LANES, LANES)
    kd = jnp.concatenate([jnp.zeros((no, 1, LANES, LANES), F32), kd], axis=1)
    tiles = []
    for delta in range(lc // 2 - 1, -1, -1):
        top = jnp.concatenate([kd[:, 2 * delta + 1], kd[:, 2 * delta + 2]], axis=-1)
        bottom = jnp.concatenate([kd[:, 2 * delta], kd[:, 2 * delta + 1]], axis=-1)
        tiles += [top, bottom]
    m1 = jnp.concatenate(tiles, axis=-2)

    er = c_re[None] * pw_r[1:, :, None, :] - c_im[None] * pw_i[1:, :, None, :]
    ei = c_re[None] * pw_i[1:, :, None, :] + c_im[None] * pw_r[1:, :, None, :]
    e2 = jnp.stack([er, -ei]).reshape(2, lc, no, OCT, h, p).transpose(2, 0, 3, 5, 1, 4)
    m2 = _group_expand(e2.reshape(no * 2, OCT * p, lc * h), p, h, 1).reshape(no, OCT_STATE, lc * LANES)

    fr = pw_r[lc - 1::-1][:lc, :, :, None] * bbr[None] - pw_i[lc - 1::-1][:lc, :, :, None] * bbi[None]
    fi = pw_r[lc - 1::-1][:lc, :, :, None] * bbi[None] + pw_i[lc - 1::-1][:lc, :, :, None] * bbr[None]
    f2 = jnp.stack([fr, fi]).reshape(2, lc, no, OCT, p, h).transpose(2, 1, 3, 5, 0, 4)
    m3 = _group_expand(f2.reshape(no * lc, OCT * h, 2 * p), h, p, 8).reshape(no, lc * LANES, OCT_STATE)

    ar = pw_r[lc].reshape(no, OCT * p)
    ai = pw_i[lc].reshape(no, OCT * p)
    a1 = jnp.tile(jnp.concatenate([ar, ar], axis=-1), (1, batch))[:, None, :]
    a2 = jnp.tile(jnp.concatenate([-ai, ai], axis=-1), (1, batch))[:, None, :]
    dsk = jnp.tile(d_skip.reshape(no, 1, LANES), (1, 1, lc))
    return m1.astype(BF16), m2.astype(BF16), m3.astype(BF16), a1, a2, dsk


def _s5_pre_kernel(x_ref, g_ref, sh_ref, sc_ref, o_ref, slab):
    tm = x_ref.shape[0]
    nt = x_ref.shape[1] // LANES
    rows = tm // CHUNK
    h = _norm_mod(x_ref[...], g_ref[...], sh_ref[...], sc_ref[...])
    for j in range(nt):
        slab[j] = h[:, j * LANES:(j + 1) * LANES]
    for j in range(nt):
        for t in range(CHUNK):
            o_ref[j, :, t * LANES:(t + 1) * LANES] = slab.at[j][pl.ds(t, rows, stride=CHUNK), :].astype(BF16)


def _s5_pre(x2, g, shift, scale, seq, tm=512):
    t, d = x2.shape
    nt = d // LANES
    per_b = seq // tm
    return pl.pallas_call(
        _s5_pre_kernel,
        grid=(t // tm,),
        in_specs=[pl.BlockSpec((tm, d), lambda i: (i, 0)),
                  pl.BlockSpec((1, d), lambda i: (0, 0)),
                  pl.BlockSpec((None, 1, d), lambda i: (i // per_b, 0, 0)),
                  pl.BlockSpec((None, 1, d), lambda i: (i // per_b, 0, 0))],
        out_specs=pl.BlockSpec((nt, tm // CHUNK, CHUNK * LANES), lambda i: (0, i, 0)),
        out_shape=jax.ShapeDtypeStruct((nt, t // CHUNK, CHUNK * LANES), BF16),
        scratch_shapes=[pltpu.VMEM((nt, tm, LANES), F32)],
        compiler_params=_cparams("parallel"),
        name="s5_pre",
    )(x2, g, shift, scale)


def _s5_state_kernel(h_ref, m3_ref, o_ref):
    o_ref[...] = _dot(h_ref[...], m3_ref[...])


def _s5_state(ho, m3, batch):
    no, r, kc = ho.shape
    c = r // batch
    return pl.pallas_call(
        _s5_state_kernel,
        grid=(no, batch),
        in_specs=[pl.BlockSpec((None, c, kc), lambda j, b: (j, b, 0)),
                  pl.BlockSpec((None, kc, OCT_STATE), lambda j, b: (j, 0, 0))],
        out_specs=pl.BlockSpec((None, c, OCT_STATE), lambda j, b: (j, 0, b)),
        out_shape=jax.ShapeDtypeStruct((no, c, batch * OCT_STATE), F32),
        compiler_params=_cparams("parallel", "parallel"),
        name="s5_state",
    )(ho, m3)


def _s5_scan_kernel(v_ref, a1_ref, a2_ref, o_ref, *, batch, unroll):
    a1 = a1_ref[...]
    a2 = a2_ref[...]
    half = OCT_STATE // 2

    def swap(s):
        parts = []
        for b in range(batch):
            parts.append(s[:, b * OCT_STATE + half:(b + 1) * OCT_STATE])
            parts.append(s[:, b * OCT_STATE:b * OCT_STATE + half])
        return jnp.concatenate(parts, axis=-1)

    def body(i, s):
        base = pl.multiple_of(i * unroll, unroll)
        v = v_ref[pl.ds(base, unroll), :]
        rows = []
        for k in range(unroll):
            rows.append(s)
            s = a1 * s + a2 * swap(s) + v[k:k + 1, :]
        o_ref[pl.ds(base, unroll), :] = jnp.concatenate(rows, axis=0).astype(o_ref.dtype)
        return s

    lax.fori_loop(0, v_ref.shape[0] // unroll, body, jnp.zeros((1, v_ref.shape[1]), F32))


def _s5_scan(v, a1, a2, batch, unroll=16):
    no, c, w = v.shape
    return pl.pallas_call(
        functools.partial(_s5_scan_kernel, batch=batch, unroll=unroll),
        grid=(no,),
        in_specs=[pl.BlockSpec((None, c, w), lambda j: (j, 0, 0)),
                  pl.BlockSpec((None, 1, w), lambda j: (j, 0, 0)),
                  pl.BlockSpec((None, 1, w), lambda j: (j, 0, 0))],
        out_specs=pl.BlockSpec((None, c, w), lambda j: (j, 0, 0)),
        out_shape=jax.ShapeDtypeStruct((no, c, w), BF16),
        compiler_params=_cparams("parallel"),
        name="s5_scan",
    )(v, a1, a2)


def _s5_out_kernel(h_ref, s_ref, m1_ref, m2_ref, d_ref, o_ref):
    tw = m1_ref.shape[1]
    n_tiles = h_ref.shape[1] // tw
    for tile in range(n_tiles):
        cols = slice(tile * tw, (tile + 1) * tw)
        y = (_dot(h_ref[:, :(tile + 1) * tw], m1_ref[(n_tiles - 1 - tile) * tw:, :])
             + _dot(s_ref[...], m2_ref[:, cols])
             + d_ref[:, cols] * h_ref[:, cols].astype(F32))
        o_ref[:, cols] = jax.nn.gelu(y).astype(BF16)


def _s5_out(ho, sin, m1, m2, dsk, batch):
    no, r, kc = ho.shape
    c = r // batch
    return pl.pallas_call(
        _s5_out_kernel,
        grid=(no, batch),
        in_specs=[pl.BlockSpec((None, c, kc), lambda j, b: (j, b, 0)),
                  pl.BlockSpec((None, c, OCT_STATE), lambda j, b: (j, 0, b)),
                  pl.BlockSpec((None, kc, m1.shape[2]), lambda j, b: (j, 0, 0)),
                  pl.BlockSpec((None, OCT_STATE, kc), lambda j, b: (j, 0, 0)),
                  pl.BlockSpec((None, 1, kc), lambda j, b: (j, 0, 0))],
        out_specs=pl.BlockSpec((None, c, kc), lambda j, b: (j, b, 0)),
        out_shape=jax.ShapeDtypeStruct((no, r, kc), BF16),
        compiler_params=_cparams("parallel", "parallel"),
        name="s5_out",
    )(ho, sin, m1, m2, dsk)


def _s5_glu_kernel(y_ref, x_ref, w_ref, b_ref, g_ref, gate_ref, o_ref, ybuf, slab):
    nt, rows, _ = y_ref.shape
    for t in range(CHUNK):
        for j in range(nt):
            ybuf[t * rows:(t + 1) * rows, j * LANES:(j + 1) * LANES] = y_ref[j, :, t * LANES:(t + 1) * LANES]
    y = ybuf[...]
    z = _dot(y, w_ref[...]) + b_ref[...]
    m = y.astype(F32) * jax.nn.sigmoid(z)
    res = (1.0 + gate_ref[...]) * (_rms(m) * g_ref[...])
    for j in range(nt):
        for t in range(CHUNK):
            slab.at[j][pl.ds(t, rows, stride=CHUNK), :] = res[t * rows:(t + 1) * rows, j * LANES:(j + 1) * LANES]
    for j in range(nt):
        o_ref[:, j * LANES:(j + 1) * LANES] = x_ref[:, j * LANES:(j + 1) * LANES] + slab[j]


def _s5_glu(yo, x2, w, b, g, gate, seq, tm=512):
    t, d = x2.shape
    nt = d // LANES
    per_b = seq // tm
    return pl.pallas_call(
        _s5_glu_kernel,
        grid=(t // tm,),
        in_specs=[pl.BlockSpec((nt, tm // CHUNK, CHUNK * LANES), lambda i: (0, i, 0)),
                  pl.BlockSpec((tm, d), lambda i: (i, 0)),
                  pl.BlockSpec((d, d), lambda i: (0, 0)),
                  pl.BlockSpec((1, d), lambda i: (0, 0)),
                  pl.BlockSpec((1, d), lambda i: (0, 0)),
                  pl.BlockSpec((None, 1, d), lambda i: (i // per_b, 0, 0))],
        out_specs=pl.BlockSpec((tm, d), lambda i: (i, 0)),
        out_shape=jax.ShapeDtypeStruct((t, d), F32),
        scratch_shapes=[pltpu.VMEM((tm, d), BF16), pltpu.VMEM((nt, tm, LANES), F32)],
        compiler_params=_cparams("parallel"),
        name="s5_glu",
    )(yo, x2, w, b, g, gate)


FFN_TF = 1024


def _swiglu_step(h, wg_ref, wu_ref, wd_ref, hid_scr):
    for c in range(hid_scr.shape[1] // LANES):
        cols = slice(c * LANES, (c + 1) * LANES)
        r = _dot(h, jnp.concatenate([wg_ref[:, cols], wu_ref[:, cols]], axis=1))
        a, u = r[:, :LANES], r[:, LANES:]
        hid_scr[:, cols] = (a * jax.nn.sigmoid(a) * u).astype(BF16)
    return _dot(hid_scr[...], wd_ref[...])


def _ffn_kernel(x_ref, g_ref, sh_ref, sc_ref, wg_ref, wu_ref, wd_ref, gp_ref, gate_ref, o_ref, h_scr, hid_scr):
    j = pl.program_id(1)

    @pl.when(j == 0)
    def _():
        h_scr[...] = _norm_mod(x_ref[...], g_ref[...], sh_ref[...], sc_ref[...]).astype(BF16)
        o_ref[...] = jnp.zeros_like(o_ref)

    o_ref[...] += _swiglu_step(h_scr[...], wg_ref, wu_ref, wd_ref, hid_scr)

    @pl.when(j == pl.num_programs(1) - 1)
    def _():
        o_ref[...] = x_ref[...] + (1.0 + gate_ref[...]) * (_rms(o_ref[...]) * gp_ref[...])


def _ffn(x2, g, shift, scale, wg, wu, wd, gp, gate, seq, tm=512):
    t, d = x2.shape
    f = wd.shape[0]
    per_b = seq // tm
    mod = pl.BlockSpec((None, 1, d), lambda i, j: (i // per_b, 0, 0))
    row = pl.BlockSpec((1, d), lambda i, j: (0, 0))
    wcol = pl.BlockSpec((d, FFN_TF), lambda i, j: (0, j))
    return pl.pallas_call(
        _ffn_kernel,
        grid=(t // tm, f // FFN_TF),
        in_specs=[pl.BlockSpec((tm, d), lambda i, j: (i, 0)), row, mod, mod, wcol, wcol,
                  pl.BlockSpec((FFN_TF, d), lambda i, j: (j, 0)),
                  row, mod],
        out_specs=pl.BlockSpec((tm, d), lambda i, j: (i, 0)),
        out_shape=jax.ShapeDtypeStruct((t, d), F32),
        scratch_shapes=[pltpu.VMEM((tm, d), BF16), pltpu.VMEM((tm, FFN_TF), BF16)],
        compiler_params=_cparams("parallel", "arbitrary"),
        name="dense_ffn",
    )(x2, g, shift, scale, wg, wu, wd, gp, gate)


def _qkv_kernel(x_ref, gq_ref, shq_ref, scq_ref, gkv_ref, shkv_ref, sckv_ref, w_ref, o_ref, hq, hkv, *, qscale):
    n = pl.program_id(1)

    @pl.when(n == 0)
    def _():
        xf = x_ref[...]
        xn = _rms(xf)
        hq[...] = (xn * (gq_ref[...] * (1.0 + scq_ref[...])) + shq_ref[...]).astype(BF16)
        hkv[...] = (xn * (gkv_ref[...] * (1.0 + sckv_ref[...])) + shkv_ref[...]).astype(BF16)
        o_ref[...] = (_dot(hq[...], w_ref[...]) * qscale).astype(BF16)

    @pl.when(n > 0)
    def _():
        o_ref[...] = _dot(hkv[...], w_ref[...]).astype(BF16)


def _qkv(x2, gq, shq, scq, gkv, shkv, sckv, w3, seq, tm=512):
    t, d = x2.shape
    per_b = seq // tm
    mod = pl.BlockSpec((None, 1, d), lambda i, n: (i // per_b, 0, 0))
    row = pl.BlockSpec((1, d), lambda i, n: (0, 0))
    qscale = math.log2(math.e) / math.sqrt(d // N_HEADS)
    return pl.pallas_call(
        functools.partial(_qkv_kernel, qscale=qscale),
        grid=(t // tm, 3),
        in_specs=[pl.BlockSpec((tm, d), lambda i, n: (i, 0)), row, mod, mod, row, mod, mod,
                  pl.BlockSpec((None, d, d), lambda i, n: (n, 0, 0))],
        out_specs=pl.BlockSpec((None, tm, d), lambda i, n: (n, i, 0)),
        out_shape=jax.ShapeDtypeStruct((3, t, d), BF16),
        scratch_shapes=[pltpu.VMEM((tm, d), BF16), pltpu.VMEM((tm, d), BF16)],
        compiler_params=_cparams("parallel", "arbitrary"),
        name="qkv_proj",
    )(x2, gq, shq, scq, gkv, shkv, sckv, w3)


ATTN_TK = 256
EXP2_ZERO_BELOW = -150.0
NO_WEIGHT = 1e30


def _attn_kernel(q_ref, k_ref, v_ref, o_ref, *, tb, tk, nh, hd):
    qi = pl.program_id(2)
    causal = (lax.broadcasted_iota(jnp.int32, (tb, tk), 1)
              < lax.broadcasted_iota(jnp.int32, (tb, tk), 0))
    krow = lax.broadcasted_iota(jnp.int32, (tk, tk), 0)
    kcol = lax.broadcasted_iota(jnp.int32, (tk, tk), 1)
    later = jnp.where(krow > kcol, 1.0, 0.0).astype(BF16)
    later2 = jnp.concatenate([later, later], axis=0)
    sign_bit = jnp.uint32(0x80000000)

    heads = range(nh)
    lanes = [slice(g * hd, (g + 1) * hd) for g in heads]

    def walk(spans, accs, runs):
        pairs = [(b, g) for b in range(len(spans)) for g in heads]
        start = [pl.multiple_of(s, tk) for s, _, _ in spans]
        y, logb, sm, suffix = {}, {}, {}, {}
        for b, g in pairs:
            y[b, g] = lax.dot_general(q_ref[:, lanes[g]], k_ref[pl.ds(start[b], tk), lanes[g]],
                                      (((1,), (1,)), ((), ())), preferred_element_type=F32)
        for b, g in pairs:
            neg_abs = lax.bitcast_convert_type(lax.bitcast_convert_type(y[b, g], jnp.uint32) | sign_bit, F32)
            s = jnp.maximum(y[b, g], 0.0) + jnp.log2(1.0 + jnp.exp2(neg_abs))
            logb[b, g] = y[b, g] - s
            sm[b, g] = jnp.where(causal, s, 0.0) if spans[b][1] else s
        for b, g in pairs:
            hi = sm[b, g].astype(BF16)
            lo = (sm[b, g] - hi.astype(F32)).astype(BF16)
            suffix[b, g] = _dot(jnp.concatenate([hi, lo], axis=1), later2)
        new_accs, new_runs = [], []
        for g in heads:
            run, acc = runs[g], accs[g]
            for b, (_, diagonal, exists) in enumerate(spans):
                if exists is not True:
                    run = jnp.where(exists, run, NO_WEIGHT)
                w = jnp.exp2((logb[b, g] - suffix[b, g]) - run)
                if diagonal:
                    w = jnp.where(causal, w, 0.0)
                acc = acc + _dot(w.astype(BF16), v_ref[pl.ds(start[b], tk), lanes[g]])
                run = run + suffix[b, g][:, 0:1] + sm[b, g][:, 0:1]
            new_accs.append(acc)
            new_runs.append(run)
        return tuple(new_accs), tuple(new_runs)

    accs = tuple(jnp.zeros((tb, hd), F32) for _ in heads)
    runs = tuple(jnp.zeros((tb, 1), F32) for _ in heads)
    start_a = qi * tb
    start_b = jnp.maximum(start_a - tk, 0)
    accs, runs = walk([(start_a, True, True), (start_b, False, qi > 0)], accs, runs)

    def cond(c):
        left, _, runs = c
        low = runs[0]
        for g in range(1, nh):
            low = jnp.minimum(low, runs[g])
        return jnp.logical_and(left > 0, jnp.min(low) <= -EXP2_ZERO_BELOW)

    def body(c):
        left, accs, runs = c
        start = left - tk
        accs, runs = walk([(start, False, True)], accs, runs)
        return start, accs, runs

    _, accs, _ = lax.while_loop(cond, body, (start_b, accs, runs))
    for g in range(nh):
        o_ref[:, g * hd:(g + 1) * hd] = accs[g].astype(o_ref.dtype)


def _attention(qkv, batch, seq, tb=256, tk=ATTN_TK, nh=4):
    _, t, d = qkv.shape
    assert tb == tk, "query tiles and key spans must have the same size"
    hd = d // N_HEADS
    nq = seq // tb
    return pl.pallas_call(
        functools.partial(_attn_kernel, tb=tb, tk=tk, nh=nh, hd=hd),
        grid=(batch, N_HEADS // nh, nq),
        in_specs=[pl.BlockSpec((None, tb, nh * hd), lambda b, h, i: (0, b * nq + i, h)),
                  pl.BlockSpec((None, seq, nh * hd), lambda b, h, i: (1, b, h)),
                  pl.BlockSpec((None, seq, nh * hd), lambda b, h, i: (2, b, h))],
        out_specs=pl.BlockSpec((tb, nh * hd), lambda b, h, i: (b * nq + i, h)),
        out_shape=jax.ShapeDtypeStruct((t, d), BF16),
        compiler_params=_cparams("parallel", "parallel", "parallel"),
        name="stickbreak_attn",
    )(qkv, qkv, qkv)


def _oproj_kernel(a_ref, x_ref, w_ref, g_ref, gate_ref, o_ref):
    m = _dot(a_ref[...], w_ref[...])
    o_ref[...] = x_ref[...] + (1.0 + gate_ref[...]) * (_rms(m) * g_ref[...])


def _oproj(a, x2, w, g, gate, seq, tm=512):
    t, d = x2.shape
    per_b = seq // tm
    return pl.pallas_call(
        _oproj_kernel,
        grid=(t // tm,),
        in_specs=[pl.BlockSpec((tm, d), lambda i: (i, 0)),
                  pl.BlockSpec((tm, d), lambda i: (i, 0)),
                  pl.BlockSpec((d, d), lambda i: (0, 0)),
                  pl.BlockSpec((1, d), lambda i: (0, 0)),
                  pl.BlockSpec((None, 1, d), lambda i: (i // per_b, 0, 0))],
        out_specs=pl.BlockSpec((tm, d), lambda i: (i, 0)),
        out_shape=jax.ShapeDtypeStruct((t, d), F32),
        compiler_params=_cparams("parallel"),
        name="attn_out_proj",
    )(a, x2, w, g, gate)


def _router_kernel(x_ref, g_ref, sh_ref, sc_ref, wr_ref, h_ref, r_ref):
    h = _norm_mod(x_ref[...], g_ref[...], sh_ref[...], sc_ref[...])
    h_ref[...] = h
    logits = jnp.dot(h, wr_ref[...], preferred_element_type=F32, precision=HIGHEST)
    lane = lax.broadcasted_iota(jnp.int32, logits.shape, 1).astype(F32)
    neg = jnp.float32(-jnp.inf)
    lg = jnp.where(lane < N_EXPERTS, logits, neg)
    m1 = jnp.max(lg, axis=-1, keepdims=True)
    i1 = jnp.min(jnp.where(lg == m1, lane, float(LANES)), axis=-1, keepdims=True)
    lg2 = jnp.where(lane == i1, neg, lg)
    m2 = jnp.max(lg2, axis=-1, keepdims=True)
    i2 = jnp.min(jnp.where(lg2 == m2, lane, float(LANES)), axis=-1, keepdims=True)
    e = jnp.exp(m2 - m1)
    p1 = 1.0 / (1.0 + e)
    p2 = e * p1
    r_ref[...] = jnp.where(lane == 0, i1, jnp.where(lane == 1, i2, jnp.where(lane == 2, p1, jnp.where(lane == 3, p2, 0.0))))


def _router(x2, g, shift, scale, wr_pad, seq, tm=512):
    t, d = x2.shape
    per_b = seq // tm
    mod = pl.BlockSpec((None, 1, d), lambda i: (i // per_b, 0, 0))
    return pl.pallas_call(
        _router_kernel,
        grid=(t // tm,),
        in_specs=[pl.BlockSpec((tm, d), lambda i: (i, 0)),
                  pl.BlockSpec((1, d), lambda i: (0, 0)), mod, mod,
                  pl.BlockSpec((d, LANES), lambda i: (0, 0))],
        out_specs=[pl.BlockSpec((tm, d), lambda i: (i, 0)),
                   pl.BlockSpec((tm, LANES), lambda i: (i, 0))],
        out_shape=[jax.ShapeDtypeStruct((t, d), F32), jax.ShapeDtypeStruct((t, LANES), F32)],
        compiler_params=_cparams("parallel"),
        name="moe_router",
    )(x2, g, shift, scale, wr_pad)


def _row_copy(src_hbm, dst, sem, src_row, dst_row):
    return pltpu.make_async_copy(src_hbm.at[pl.ds(src_row, 1)], dst.at[pl.ds(dst_row, 1)], sem)


def _moe_kernel(te_ref, nu_ref, tok_ref, nxt_ref, h_hbm, wg_ref, wu_ref, wd_ref, o_ref, xbuf, xb16, hid_scr, sem):
    i = pl.program_id(0)
    j = pl.program_id(1)
    tm = xb16.shape[0]
    used = i < nu_ref[0]
    slot = lax.rem(i, 2)

    def gather(idx_ref, s):
        def start(r, c):
            _row_copy(h_hbm, xbuf.at[s], sem.at[s], idx_ref[0, r], r).start()
            return c
        lax.fori_loop(0, tm, start, 0, unroll=8)

    @pl.when(jnp.logical_and(i == 0, j == 0))
    def _():
        gather(tok_ref, 0)

    @pl.when(jnp.logical_and(i + 1 < nu_ref[0], j == 0))
    def _():
        gather(nxt_ref, 1 - slot)

    @pl.when(jnp.logical_and(used, j == 0))
    def _():
        def wait(r, c):
            _row_copy(h_hbm, xbuf.at[slot], sem.at[slot], 0, r).wait()
            return c
        lax.fori_loop(0, tm, wait, 0, unroll=8)
        xb16[...] = xbuf[slot].astype(BF16)

    @pl.when(j == 0)
    def _():
        o_ref[...] = jnp.zeros_like(o_ref)

    @pl.when(used)
    def _():
        o_ref[...] += _swiglu_step(xb16[...], wg_ref, wu_ref, wd_ref, hid_scr)


def _moe_experts(tile_expert, n_used, sorted_tok, h, wg, wu, wd, tm=512):
    t, d = h.shape
    f = wd.shape[1]
    nt = sorted_tok.shape[0]
    nf = f // FFN_TF

    def wcol(i, j, te, nu):
        return (te[i], 0, jnp.where(i < nu[0], j, nf - 1))

    def wrow(i, j, te, nu):
        return (te[i], jnp.where(i < nu[0], j, nf - 1), 0)

    grid_spec = pltpu.PrefetchScalarGridSpec(
        num_scalar_prefetch=2,
        grid=(nt, nf),
        in_specs=[pl.BlockSpec((None, 1, tm), lambda i, j, te, nu: (i, 0, 0), memory_space=pltpu.SMEM),
                  pl.BlockSpec((None, 1, tm), lambda i, j, te, nu: (jnp.minimum(i + 1, nt - 1), 0, 0),
                               memory_space=pltpu.SMEM),
                  pl.BlockSpec(memory_space=pl.ANY),
                  pl.BlockSpec((None, d, FFN_TF), wcol),
                  pl.BlockSpec((None, d, FFN_TF), wcol),
                  pl.BlockSpec((None, FFN_TF, d), wrow)],
        out_specs=pl.BlockSpec((tm, d), lambda i, j, te, nu: (i, 0)),
        scratch_shapes=[pltpu.VMEM((2, tm, d), F32), pltpu.VMEM((tm, d), BF16), pltpu.VMEM((tm, FFN_TF), BF16),
                        pltpu.SemaphoreType.DMA((2,))],
    )
    return pl.pallas_call(
        _moe_kernel,
        grid_spec=grid_spec,
        out_shape=jax.ShapeDtypeStruct((nt * tm, d), F32),
        compiler_params=_cparams("arbitrary", "arbitrary"),
        name="moe_experts",
    )(tile_expert, n_used, sorted_tok, sorted_tok, h, wg, wu, wd)


def _combine_kernel(d0_ref, d1_ref, n0_ref, n1_ref, y_hbm, r_ref, x_ref, g_ref, gate_ref, o_ref, buf, sem):
    i = pl.program_id(0)
    tm = x_ref.shape[0]
    slot = lax.rem(i, 2)

    def gather(i0_ref, i1_ref, s):
        def start(r, c):
            _row_copy(y_hbm, buf.at[s, 0], sem.at[s], i0_ref[0, r], r).start()
            _row_copy(y_hbm, buf.at[s, 1], sem.at[s], i1_ref[0, r], r).start()
            return c
        lax.fori_loop(0, tm, start, 0, unroll=8)

    @pl.when(i == 0)
    def _():
        gather(d0_ref, d1_ref, 0)

    @pl.when(i + 1 < pl.num_programs(0))
    def _():
        gather(n0_ref, n1_ref, 1 - slot)

    def wait(r, c):
        _row_copy(y_hbm, buf.at[slot, 0], sem.at[slot], 0, r).wait()
        _row_copy(y_hbm, buf.at[slot, 1], sem.at[slot], 0, r).wait()
        return c

    lax.fori_loop(0, tm, wait, 0, unroll=8)
    r = r_ref[...]
    f = r[:, 2:3] * buf[slot, 0] + r[:, 3:4] * buf[slot, 1]
    o_ref[...] = x_ref[...] + (1.0 + gate_ref[...]) * (_rms(f) * g_ref[...])


def _moe_combine(dest0, dest1, y, route, x2, g, gate, seq, tm=256):
    t, d = x2.shape
    per_b = seq // tm
    nt = t // tm
    idx = pl.BlockSpec((None, 1, tm), lambda i: (i, 0, 0), memory_space=pltpu.SMEM)
    nxt = pl.BlockSpec((None, 1, tm), lambda i: (jnp.minimum(i + 1, nt - 1), 0, 0), memory_space=pltpu.SMEM)
    d0 = dest0.reshape(nt, 1, tm)
    d1 = dest1.reshape(nt, 1, tm)
    return pl.pallas_call(
        _combine_kernel,
        grid=(nt,),
        in_specs=[idx, idx, nxt, nxt,
                  pl.BlockSpec(memory_space=pl.ANY),
                  pl.BlockSpec((tm, LANES), lambda i: (i, 0)),
                  pl.BlockSpec((tm, d), lambda i: (i, 0)),
                  pl.BlockSpec((1, d), lambda i: (0, 0)),
                  pl.BlockSpec((None, 1, d), lambda i: (i // per_b, 0, 0))],
        out_specs=pl.BlockSpec((tm, d), lambda i: (i, 0)),
        out_shape=jax.ShapeDtypeStruct((t, d), F32),
        scratch_shapes=[pltpu.VMEM((2, 2, tm, d), F32), pltpu.SemaphoreType.DMA((2,))],
        compiler_params=_cparams("arbitrary"),
        name="moe_combine",
    )(d0, d1, d0, d1, y, route, x2, g, gate)


def _routing_plan(route, tm):
    t = route.shape[0]
    ex = route[:, 0:2].astype(jnp.int32).T.reshape(-1)
    onehot = (ex[:, None] == jnp.arange(N_EXPERTS, dtype=jnp.int32)[None, :]).astype(jnp.int32)
    csum = jnp.cumsum(onehot, axis=0)
    rank = jnp.sum(onehot * csum, axis=1) - 1
    counts = csum[-1]
    tiles = (counts + tm - 1) // tm
    tile_end = jnp.cumsum(tiles)
    offset = (tile_end - tiles) * tm
    dest = offset[ex] + rank
    nt = (2 * t) // tm + N_EXPERTS
    n_used = tile_end[-1]
    tile_ids = jnp.minimum(jnp.arange(nt, dtype=jnp.int32), n_used - 1)
    tile_expert = jnp.sum((tile_ids[:, None] >= tile_end[None, :]).astype(jnp.int32), axis=1)
    token = jnp.arange(2 * t, dtype=jnp.int32) % t
    sorted_tok = jnp.zeros((nt * tm,), jnp.int32).at[dest].set(token)
    return (tile_expert.astype(jnp.int32), n_used.reshape(1).astype(jnp.int32),
            sorted_tok.reshape(nt, 1, tm), dest[:t], dest[t:])


def kernel(x, c, ada_mix_w, ada_mix_b, norm_mix_pre, norm_mix_post, ada_ffn_w, ada_ffn_b, norm_ffn_pre, norm_ffn_post, s5_a_re, s5_a_im, s5_b_re, s5_b_im, s5_c_re, s5_c_im, s5_d, s5_log_step, s5_glu_w, s5_glu_b, kv_ada_w, kv_ada_b, kv_norm, w_k, w_v, w_q, w_o, ffn_w_gate, ffn_w_up, ffn_w_down, moe_w_router, moe_w_gate, moe_w_up, moe_w_down):
    bsz, seq, d = x.shape
    x2 = x.reshape(bsz * seq, d)

    c_pad = jnp.zeros((8, d), F32).at[:bsz].set(c)
    mix = _ada(c_pad, ada_mix_w, ada_mix_b)
    ffn = _ada(c_pad, ada_ffn_w, ada_ffn_b)
    kvm = _ada(c_pad, kv_ada_w[None], kv_ada_b[None])

    def mods(m, layer, n):
        return [m[layer, :bsz, k * d:(k + 1) * d].reshape(bsz, 1, d) for k in range(n)]

    def row(v):
        return v.reshape(1, d)

    shift, scale, gate = mods(mix, 0, 3)
    m1, m2, m3, a1, a2, dsk = _s5_tables(s5_a_re[0], s5_a_im[0], s5_b_re[0], s5_b_im[0],
                                         s5_c_re[0], s5_c_im[0], s5_log_step[0], s5_d[0], bsz)
    ho = _s5_pre(x2, row(norm_mix_pre[0]), shift, scale, seq)
    v = _s5_state(ho, m3, bsz)
    sin = _s5_scan(v, a1, a2, bsz)
    yo = _s5_out(ho, sin, m1, m2, dsk, bsz)
    x2 = _s5_glu(yo, x2, s5_glu_w[0].astype(BF16), row(s5_glu_b[0]), row(norm_mix_post[0]), gate, seq)

    shift, scale, gate = mods(ffn, 0, 3)
    x2 = _ffn(x2, row(norm_ffn_pre[0]), shift, scale, ffn_w_gate[0].astype(BF16), ffn_w_up[0].astype(BF16),
              ffn_w_down[0].astype(BF16), row(norm_ffn_post[0]), gate, seq)

    shift, scale, gate = mods(mix, 1, 3)
    kv_shift, kv_scale = mods(kvm, 0, 2)
    w3 = jnp.stack([w_q[0], w_k, w_v]).astype(BF16)
    qkv = _qkv(x2, row(norm_mix_pre[1]), shift, scale, row(kv_norm), kv_shift, kv_scale, w3, seq)
    att = _attention(qkv, bsz, seq)
    x2 = _oproj(att, x2, w_o[0].astype(BF16), row(norm_mix_post[1]), gate, seq)

    shift, scale, gate = mods(ffn, 1, 3)
    wr_pad = jnp.zeros((d, LANES), F32).at[:, :N_EXPERTS].set(moe_w_router[0])
    h, route = _router(x2, row(norm_ffn_pre[1]), shift, scale, wr_pad, seq)
    tm_moe = 512
    tile_expert, n_used, sorted_tok, dest0, dest1 = _routing_plan(route, tm_moe)
    y = _moe_experts(tile_expert, n_used, sorted_tok, h, moe_w_gate[0].astype(BF16), moe_w_up[0].astype(BF16),
                     moe_w_down[0].astype(BF16), tm=tm_moe)
    x2 = _moe_combine(dest0, dest1, y, route, x2, row(norm_ffn_post[1]), gate, seq)
    return x2.reshape(bsz, seq, d)
```

```python
import functools
import math

import jax
import jax.numpy as jnp
from jax import lax
from jax.experimental import pallas as pl
from jax.experimental.pallas import tpu as pltpu

F32 = jnp.float32
BF16 = jnp.bfloat16

LANES = 128
SUBLANES = 8
VMEM_LIMIT = 56 * 1024 * 1024
RMS_EPS = 1e-6

S5_GROUP = 16
S5_STATE = 64
CHUNK = 16
OCT = LANES // S5_GROUP
OCT_STATE = 2 * OCT * S5_STATE
N_HEADS = 16
N_EXPERTS = 8
HIGHEST = lax.Precision.HIGHEST


def _cparams(*semantics):
    return pltpu.CompilerParams(dimension_semantics=semantics, vmem_limit_bytes=VMEM_LIMIT)


def _dot(a, b):
    return jnp.dot(a, b, preferred_element_type=F32)


def _rms(xf):
    ms = jnp.mean(xf * xf, axis=-1, keepdims=True)
    return xf * lax.rsqrt(ms + RMS_EPS)


def _norm_mod(xf, g, shift, scale):
    return _rms(xf) * (g * (1.0 + scale)) + shift


def _ada_kernel(c_ref, w_ref, b_ref, o_ref):
    s = c_ref[...]
    s = s * jax.nn.sigmoid(s)
    o_ref[...] = _dot(s.astype(BF16), w_ref[...].astype(BF16)) + b_ref[...]


def _ada(c_pad, w, b, tn=1024):
    nl, d, n = w.shape
    return pl.pallas_call(
        _ada_kernel,
        grid=(nl, n // tn),
        in_specs=[pl.BlockSpec((SUBLANES, d), lambda l, j: (0, 0)),
                  pl.BlockSpec((None, d, tn), lambda l, j: (l, 0, j)),
                  pl.BlockSpec((None, 1, tn), lambda l, j: (l, 0, j))],
        out_specs=pl.BlockSpec((None, SUBLANES, tn), lambda l, j: (l, 0, j)),
        out_shape=jax.ShapeDtypeStruct((nl, SUBLANES, n), F32),
        compiler_params=_cparams("parallel", "parallel"),
        name="ada_mods",
    )(c_pad, w, b.reshape(nl, 1, n))


def _expand_kernel(x_ref, rep_ref, o_ref, *, row_shift, col_shift):
    nb, rows, k = x_ref.shape
    n = o_ref.shape[2]
    x = x_ref[...].reshape(nb * rows, k).astype(BF16)
    spread = _dot(x, rep_ref[...])
    group = lax.shift_right_logical(lax.broadcasted_iota(jnp.int32, (nb * rows, n), 0), row_shift) & (OCT - 1)
    slot = lax.shift_right_logical(lax.broadcasted_iota(jnp.int32, (nb * rows, n), 1), col_shift) & (OCT - 1)
    o_ref[...] = jnp.where(group == slot, spread, 0.0).astype(BF16).reshape(nb, rows, n)


def _group_expand(x, rows_per_group, inner, nb):
    b, rows, k = x.shape
    n = k * OCT
    kk = jnp.arange(k, dtype=jnp.int32)
    nn = jnp.arange(n, dtype=jnp.int32)
    rep = jnp.logical_and((kk // inner)[:, None] == (nn // (inner * OCT))[None, :],
                          (kk % inner)[:, None] == (nn % inner)[None, :]).astype(BF16)
    return pl.pallas_call(
        functools.partial(_expand_kernel, row_shift=rows_per_group.bit_length() - 1,
                          col_shift=inner.bit_length() - 1),
        grid=(b // nb,),
        in_specs=[pl.BlockSpec((nb, rows, k), lambda i: (i, 0, 0)),
                  pl.BlockSpec((k, n), lambda i: (0, 0))],
        out_specs=pl.BlockSpec((nb, rows, n), lambda i: (i, 0, 0)),
        out_shape=jax.ShapeDtypeStruct((b, rows, n), BF16),
        compiler_params=_cparams("parallel"),
        name="s5_table_expand",
    )(x, rep)


def _s5_tables(a_re, a_im, b_re, b_im, c_re, c_im, log_step, d_skip, batch):
    g, p = a_re.shape
    h, lc = S5_GROUP, CHUNK
    no = g // OCT
    dt = jnp.exp(log_step)[:, None]
    lr, li = a_re * dt, a_im * dt
    k = jnp.arange(lc + 1, dtype=F32)[:, None, None]
    mag = jnp.exp(k * lr[None])
    pw_r, pw_i = mag * jnp.cos(k * li[None]), mag * jnp.sin(k * li[None])
    nr, ni = pw_r[1] - 1.0, pw_i[1]
    den = a_re * a_re + a_im * a_im
    qr, qi = (nr * a_re + ni * a_im) / den, (ni * a_re - nr * a_im) / den
    bbr = qr[..., None] * b_re - qi[..., None] * b_im
    bbi = qr[..., None] * b_im + qi[..., None] * b_re
    same_group = jnp.eye(OCT, dtype=F32).reshape(1, 1, OCT, 1, 1, OCT, 1)

    wr, wi = pw_r[:lc, :, None, :], pw_i[:lc, :, None, :]
    cwr = c_re[None] * wr - c_im[None] * wi
    cwi = c_re[None] * wi + c_im[None] * wr
    kk = jnp.einsum('tgop,gpi->tgio', jnp.concatenate([cwr, -cwi], axis=-1),
                    jnp.concatenate([bbr, bbi], axis=1), precision=HIGHEST)
    kk = kk.reshape(lc, no, OCT, h, h).transpose(1, 0, 2, 3, 4)
    kd = (kk[:, :, :, :, None, None, :] * same_group).reshape(no, lc, LANES, LANES)
    kd = jnp.concatenate([jnp.zeros((no, 1, LANES, LANES), F32), kd], axis=1)
    tiles = []
    for delta in range(lc // 2 - 1, -1, -1):
        top = jnp.concatenate([kd[:, 2 * delta + 1], kd[:, 2 * delta + 2]], axis=-1)
        bottom = jnp.concatenate([kd[:, 2 * delta], kd[:, 2 * delta + 1]], axis=-1)
        tiles += [top, bottom]
    m1 = jnp.concatenate(tiles, axis=-2)

    er = c_re[None] * pw_r[1:, :, None, :] - c_im[None] * pw_i[1:, :, None, :]
    ei = c_re[None] * pw_i[1:, :, None, :] + c_im[None] * pw_r[1:, :, None, :]
    e2 = jnp.stack([er, -ei]).reshape(2, lc, no, OCT, h, p).transpose(2, 0, 3, 5, 1, 4)
    m2 = _group_expand(e2.reshape(no * 2, OCT * p, lc * h), p, h, nb=1).reshape(no, OCT_STATE, lc * LANES)

    fr = pw_r[lc - 1::-1][:lc, :, :, None] * bbr[None] - pw_i[lc - 1::-1][:lc, :, :, None] * bbi[None]
    fi = pw_r[lc - 1::-1][:lc, :, :, None] * bbi[None] + pw_i[lc - 1::-1][:lc, :, :, None] * bbr[None]
    f2 = jnp.stack([fr, fi]).reshape(2, lc, no, OCT, p, h).transpose(2, 1, 3, 5, 0, 4)
    m3 = _group_expand(f2.reshape(no * lc, OCT * h, 2 * p), h, p, nb=8).reshape(no, lc * LANES, OCT_STATE)

    ar = pw_r[lc].reshape(no, OCT * p)
    ai = pw_i[lc].reshape(no, OCT * p)
    a1 = jnp.tile(jnp.concatenate([ar, ar], axis=-1), (1, batch))[:, None, :]
    a2 = jnp.tile(jnp.concatenate([-ai, ai], axis=-1), (1, batch))[:, None, :]
    dsk = jnp.tile(d_skip.reshape(no, 1, LANES), (1, 1, lc))
    return m1.astype(BF16), m2.astype(BF16), m3.astype(BF16), a1, a2, dsk


def _s5_pre_kernel(x_ref, g_ref, sh_ref, sc_ref, o_ref, slab):
    tm = x_ref.shape[0]
    nt = x_ref.shape[1] // LANES
    rows = tm // CHUNK
    h = _norm_mod(x_ref[...], g_ref[...], sh_ref[...], sc_ref[...])
    for j in range(nt):
        slab[j] = h[:, j * LANES:(j + 1) * LANES]
    for j in range(nt):
        for t in range(CHUNK):
            o_ref[j, :, t * LANES:(t + 1) * LANES] = slab.at[j][pl.ds(t, rows, stride=CHUNK), :].astype(BF16)


def _s5_pre(x2, g, shift, scale, seq, tm=512):
    t, d = x2.shape
    nt = d // LANES
    per_b = seq // tm
    return pl.pallas_call(
        _s5_pre_kernel,
        grid=(t // tm,),
        in_specs=[pl.BlockSpec((tm, d), lambda i: (i, 0)),
                  pl.BlockSpec((1, d), lambda i: (0, 0)),
                  pl.BlockSpec((None, 1, d), lambda i: (i // per_b, 0, 0)),
                  pl.BlockSpec((None, 1, d), lambda i: (i // per_b, 0, 0))],
        out_specs=pl.BlockSpec((nt, tm // CHUNK, CHUNK * LANES), lambda i: (0, i, 0)),
        out_shape=jax.ShapeDtypeStruct((nt, t // CHUNK, CHUNK * LANES), BF16),
        scratch_shapes=[pltpu.VMEM((nt, tm, LANES), F32)],
        compiler_params=_cparams("parallel"),
        name="s5_pre",
    )(x2, g, shift, scale)


def _s5_state_kernel(h_ref, m3_ref, o_ref):
    o_ref[...] = _dot(h_ref[...], m3_ref[...])


def _s5_state(ho, m3, batch):
    no, r, kc = ho.shape
    c = r // batch
    return pl.pallas_call(
        _s5_state_kernel,
        grid=(no, batch),
        in_specs=[pl.BlockSpec((None, c, kc), lambda j, b: (j, b, 0)),
                  pl.BlockSpec((None, kc, OCT_STATE), lambda j, b: (j, 0, 0))],
        out_specs=pl.BlockSpec((None, c, OCT_STATE), lambda j, b: (j, 0, b)),
        out_shape=jax.ShapeDtypeStruct((no, c, batch * OCT_STATE), F32),
        compiler_params=_cparams("parallel", "parallel"),
        name="s5_state",
    )(ho, m3)


def _s5_scan_kernel(v_ref, a1_ref, a2_ref, o_ref, *, batch, unroll):
    a1 = a1_ref[...]
    a2 = a2_ref[...]
    half = OCT_STATE // 2

    def swap(s):
        parts = []
        for b in range(batch):
            parts.append(s[:, b * OCT_STATE + half:(b + 1) * OCT_STATE])
            parts.append(s[:, b * OCT_STATE:b * OCT_STATE + half])
        return jnp.concatenate(parts, axis=-1)

    def body(i, s):
        base = pl.multiple_of(i * unroll, unroll)
        v = v_ref[pl.ds(base, unroll), :]
        rows = []
        for k in range(unroll):
            rows.append(s)
            s = a1 * s + a2 * swap(s) + v[k:k + 1, :]
        o_ref[pl.ds(base, unroll), :] = jnp.concatenate(rows, axis=0).astype(o_ref.dtype)
        return s

    lax.fori_loop(0, v_ref.shape[0] // unroll, body, jnp.zeros((1, v_ref.shape[1]), F32))


def _s5_scan(v, a1, a2, batch, unroll=16):
    no, c, w = v.shape
    return pl.pallas_call(
        functools.partial(_s5_scan_kernel, batch=batch, unroll=unroll),
        grid=(no,),
        in_specs=[pl.BlockSpec((None, c, w), lambda j: (j, 0, 0)),
                  pl.BlockSpec((None, 1, w), lambda j: (j, 0, 0)),
                  pl.BlockSpec((None, 1, w), lambda j: (j, 0, 0))],
        out_specs=pl.BlockSpec((None, c, w), lambda j: (j, 0, 0)),
        out_shape=jax.ShapeDtypeStruct((no, c, w), BF16),
        compiler_params=_cparams("parallel"),
        name="s5_scan",
    )(v, a1, a2)


def _s5_out_kernel(h_ref, s_ref, m1_ref, m2_ref, d_ref, o_ref):
    tw = m1_ref.shape[1]
    n_tiles = h_ref.shape[1] // tw
    for tile in range(n_tiles):
        cols = slice(tile * tw, (tile + 1) * tw)
        y = (_dot(h_ref[:, :(tile + 1) * tw], m1_ref[(n_tiles - 1 - tile) * tw:, :])
             + _dot(s_ref[...], m2_ref[:, cols])
             + d_ref[:, cols] * h_ref[:, cols].astype(F32))
        o_ref[:, cols] = jax.nn.gelu(y).astype(BF16)


def _s5_out(ho, sin, m1, m2, dsk, batch):
    no, r, kc = ho.shape
    c = r // batch
    return pl.pallas_call(
        _s5_out_kernel,
        grid=(no, batch),
        in_specs=[pl.BlockSpec((None, c, kc), lambda j, b: (j, b, 0)),
                  pl.BlockSpec((None, c, OCT_STATE), lambda j, b: (j, 0, b)),
                  pl.BlockSpec((None, kc, m1.shape[2]), lambda j, b: (j, 0, 0)),
                  pl.BlockSpec((None, OCT_STATE, kc), lambda j, b: (j, 0, 0)),
                  pl.BlockSpec((None, 1, kc), lambda j, b: (j, 0, 0))],
        out_specs=pl.BlockSpec((None, c, kc), lambda j, b: (j, b, 0)),
        out_shape=jax.ShapeDtypeStruct((no, r, kc), BF16),
        compiler_params=_cparams("parallel", "parallel"),
        name="s5_out",
    )(ho, sin, m1, m2, dsk)


def _s5_glu_kernel(y_ref, x_ref, w_ref, b_ref, g_ref, gate_ref, o_ref, ybuf, slab):
    nt, rows, _ = y_ref.shape
    for t in range(CHUNK):
        for j in range(nt):
            ybuf[t * rows:(t + 1) * rows, j * LANES:(j + 1) * LANES] = y_ref[j, :, t * LANES:(t + 1) * LANES]
    y = ybuf[...]
    z = _dot(y, w_ref[...]) + b_ref[...]
    m = y.astype(F32) * jax.nn.sigmoid(z)
    res = (1.0 + gate_ref[...]) * (_rms(m) * g_ref[...])
    for j in range(nt):
        for t in range(CHUNK):
            slab.at[j][pl.ds(t, rows, stride=CHUNK), :] = res[t * rows:(t + 1) * rows, j * LANES:(j + 1) * LANES]
    for j in range(nt):
        o_ref[:, j * LANES:(j + 1) * LANES] = x_ref[:, j * LANES:(j + 1) * LANES] + slab[j]


def _s5_glu(yo, x2, w, b, g, gate, seq, tm=512):
    t, d = x2.shape
    nt = d // LANES
    per_b = seq // tm
    return pl.pallas_call(
        _s5_glu_kernel,
        grid=(t // tm,),
        in_specs=[pl.BlockSpec((nt, tm // CHUNK, CHUNK * LANES), lambda i: (0, i, 0)),
                  pl.BlockSpec((tm, d), lambda i: (i, 0)),
                  pl.BlockSpec((d, d), lambda i: (0, 0)),
                  pl.BlockSpec((1, d), lambda i: (0, 0)),
                  pl.BlockSpec((1, d), lambda i: (0, 0)),
                  pl.BlockSpec((None, 1, d), lambda i: (i // per_b, 0, 0))],
        out_specs=pl.BlockSpec((tm, d), lambda i: (i, 0)),
        out_shape=jax.ShapeDtypeStruct((t, d), F32),
        scratch_shapes=[pltpu.VMEM((tm, d), BF16), pltpu.VMEM((nt, tm, LANES), F32)],
        compiler_params=_cparams("parallel"),
        name="s5_glu",
    )(yo, x2, w, b, g, gate)


FFN_TF = 1024
MOE_TM = 512


def _swiglu_step(h, wg_ref, wu_ref, wd_ref, hid_scr):
    for c in range(hid_scr.shape[1] // LANES):
        cols = slice(c * LANES, (c + 1) * LANES)
        r = _dot(h, jnp.concatenate([wg_ref[:, cols], wu_ref[:, cols]], axis=1))
        a, u = r[:, :LANES], r[:, LANES:]
        hid_scr[:, cols] = (a * jax.nn.sigmoid(a) * u).astype(BF16)
    return _dot(hid_scr[...], wd_ref[...])


def _ffn_kernel(x_ref, g_ref, sh_ref, sc_ref, wg_ref, wu_ref, wd_ref, gp_ref, gate_ref, o_ref, h_scr, hid_scr):
    j = pl.program_id(1)

    @pl.when(j == 0)
    def _():
        h_scr[...] = _norm_mod(x_ref[...], g_ref[...], sh_ref[...], sc_ref[...]).astype(BF16)
        o_ref[...] = jnp.zeros_like(o_ref)

    o_ref[...] += _swiglu_step(h_scr[...], wg_ref, wu_ref, wd_ref, hid_scr)

    @pl.when(j == pl.num_programs(1) - 1)
    def _():
        o_ref[...] = x_ref[...] + (1.0 + gate_ref[...]) * (_rms(o_ref[...]) * gp_ref[...])


def _ffn(x2, g, shift, scale, wg, wu, wd, gp, gate, seq, tm=512):
    t, d = x2.shape
    f = wd.shape[0]
    per_b = seq // tm
    mod = pl.BlockSpec((None, 1, d), lambda i, j: (i // per_b, 0, 0))
    row = pl.BlockSpec((1, d), lambda i, j: (0, 0))
    wcol = pl.BlockSpec((d, FFN_TF), lambda i, j: (0, j))
    return pl.pallas_call(
        _ffn_kernel,
        grid=(t // tm, f // FFN_TF),
        in_specs=[pl.BlockSpec((tm, d), lambda i, j: (i, 0)), row, mod, mod, wcol, wcol,
                  pl.BlockSpec((FFN_TF, d), lambda i, j: (j, 0)),
                  row, mod],
        out_specs=pl.BlockSpec((tm, d), lambda i, j: (i, 0)),
        out_shape=jax.ShapeDtypeStruct((t, d), F32),
        scratch_shapes=[pltpu.VMEM((tm, d), BF16), pltpu.VMEM((tm, FFN_TF), BF16)],
        compiler_params=_cparams("parallel", "arbitrary"),
        name="dense_ffn",
    )(x2, g, shift, scale, wg, wu, wd, gp, gate)


def _qkv_kernel(x_ref, gq_ref, shq_ref, scq_ref, gkv_ref, shkv_ref, sckv_ref, w_ref, o_ref, hq, hkv, *, qscale):
    n = pl.program_id(1)

    @pl.when(n == 0)
    def _():
        xf = x_ref[...]
        xn = _rms(xf)
        hq[...] = (xn * (gq_ref[...] * (1.0 + scq_ref[...])) + shq_ref[...]).astype(BF16)
        hkv[...] = (xn * (gkv_ref[...] * (1.0 + sckv_ref[...])) + shkv_ref[...]).astype(BF16)
        o_ref[...] = (_dot(hq[...], w_ref[...]) * qscale).astype(BF16)

    @pl.when(n > 0)
    def _():
        o_ref[...] = _dot(hkv[...], w_ref[...]).astype(BF16)


def _qkv(x2, gq, shq, scq, gkv, shkv, sckv, w3, seq, tm=512):
    t, d = x2.shape
    per_b = seq // tm
    mod = pl.BlockSpec((None, 1, d), lambda i, n: (i // per_b, 0, 0))
    row = pl.BlockSpec((1, d), lambda i, n: (0, 0))
    qscale = math.log2(math.e) / math.sqrt(d // N_HEADS)
    return pl.pallas_call(
        functools.partial(_qkv_kernel, qscale=qscale),
        grid=(t // tm, 3),
        in_specs=[pl.BlockSpec((tm, d), lambda i, n: (i, 0)), row, mod, mod, row, mod, mod,
                  pl.BlockSpec((None, d, d), lambda i, n: (n, 0, 0))],
        out_specs=pl.BlockSpec((None, tm, d), lambda i, n: (n, i, 0)),
        out_shape=jax.ShapeDtypeStruct((3, t, d), BF16),
        scratch_shapes=[pltpu.VMEM((tm, d), BF16), pltpu.VMEM((tm, d), BF16)],
        compiler_params=_cparams("parallel", "arbitrary"),
        name="qkv_proj",
    )(x2, gq, shq, scq, gkv, shkv, sckv, w3)


ATTN_TK = 256
EXP2_ZERO_BELOW = -150.0
NO_WEIGHT = 1e30


def _attn_kernel(q_ref, k_ref, v_ref, o_ref, *, tb, tk, nh, hd):
    qi = pl.program_id(2)
    causal = (lax.broadcasted_iota(jnp.int32, (tb, tk), 1)
              < lax.broadcasted_iota(jnp.int32, (tb, tk), 0))
    krow = lax.broadcasted_iota(jnp.int32, (tk, tk), 0)
    kcol = lax.broadcasted_iota(jnp.int32, (tk, tk), 1)
    later = jnp.where(krow > kcol, 1.0, 0.0).astype(BF16)
    later2 = jnp.concatenate([later, later], axis=0)
    sign_bit = jnp.uint32(0x80000000)

    heads = range(nh)
    lanes = [slice(g * hd, (g + 1) * hd) for g in heads]

    def walk(spans, accs, runs):
        pairs = [(b, g) for b in range(len(spans)) for g in heads]
        start = [pl.multiple_of(s, tk) for s, _, _ in spans]
        y, logb, sm, suffix = {}, {}, {}, {}
        for b, g in pairs:
            y[b, g] = lax.dot_general(q_ref[:, lanes[g]], k_ref[pl.ds(start[b], tk), lanes[g]],
                                      (((1,), (1,)), ((), ())), preferred_element_type=F32)
        for b, g in pairs:
            neg_abs = lax.bitcast_convert_type(lax.bitcast_convert_type(y[b, g], jnp.uint32) | sign_bit, F32)
            s = jnp.maximum(y[b, g], 0.0) + jnp.log2(1.0 + jnp.exp2(neg_abs))
            logb[b, g] = y[b, g] - s
            sm[b, g] = jnp.where(causal, s, 0.0) if spans[b][1] else s
        for b, g in pairs:
            hi = sm[b, g].astype(BF16)
            lo = (sm[b, g] - hi.astype(F32)).astype(BF16)
            suffix[b, g] = _dot(jnp.concatenate([hi, lo], axis=1), later2)
        new_accs, new_runs = [], []
        for g in heads:
            run, acc = runs[g], accs[g]
            for b, (_, diagonal, exists) in enumerate(spans):
                if exists is not True:
                    run = jnp.where(exists, run, NO_WEIGHT)
                w = jnp.exp2((logb[b, g] - suffix[b, g]) - run)
                if diagonal:
                    w = jnp.where(causal, w, 0.0)
                acc = acc + _dot(w.astype(BF16), v_ref[pl.ds(start[b], tk), lanes[g]])
                run = run + suffix[b, g][:, 0:1] + sm[b, g][:, 0:1]
            new_accs.append(acc)
            new_runs.append(run)
        return tuple(new_accs), tuple(new_runs)

    accs = tuple(jnp.zeros((tb, hd), F32) for _ in heads)
    runs = tuple(jnp.zeros((tb, 1), F32) for _ in heads)
    start_a = qi * tb
    start_b = jnp.maximum(start_a - tk, 0)
    accs, runs = walk([(start_a, True, True), (start_b, False, qi > 0)], accs, runs)

    def cond(c):
        left, _, runs = c
        low = runs[0]
        for g in range(1, nh):
            low = jnp.minimum(low, runs[g])
        return jnp.logical_and(left > 0, jnp.min(low) <= -EXP2_ZERO_BELOW)

    def body(c):
        left, accs, runs = c
        start = left - tk
        accs, runs = walk([(start, False, True)], accs, runs)
        return start, accs, runs

    _, accs, _ = lax.while_loop(cond, body, (start_b, accs, runs))
    for g in range(nh):
        o_ref[:, g * hd:(g + 1) * hd] = accs[g].astype(o_ref.dtype)


def _attention(qkv, batch, seq, tb=256, tk=ATTN_TK, nh=4):
    _, t, d = qkv.shape
    assert tb == tk, "query tiles and key spans must have the same size"
    hd = d // N_HEADS
    nq = seq // tb
    return pl.pallas_call(
        functools.partial(_attn_kernel, tb=tb, tk=tk, nh=nh, hd=hd),
        grid=(batch, N_HEADS // nh, nq),
        in_specs=[pl.BlockSpec((None, tb, nh * hd), lambda b, h, i: (0, b * nq + i, h)),
                  pl.BlockSpec((None, seq, nh * hd), lambda b, h, i: (1, b, h)),
                  pl.BlockSpec((None, seq, nh * hd), lambda b, h, i: (2, b, h))],
        out_specs=pl.BlockSpec((tb, nh * hd), lambda b, h, i: (b * nq + i, h)),
        out_shape=jax.ShapeDtypeStruct((t, d), BF16),
        compiler_params=_cparams("parallel", "parallel", "parallel"),
        name="stickbreak_attn",
    )(qkv, qkv, qkv)


def _oproj_kernel(a_ref, x_ref, w_ref, g_ref, gate_ref, o_ref):
    m = _dot(a_ref[...], w_ref[...])
    o_ref[...] = x_ref[...] + (1.0 + gate_ref[...]) * (_rms(m) * g_ref[...])


def _oproj(a, x2, w, g, gate, seq, tm=512):
    t, d = x2.shape
    per_b = seq // tm
    return pl.pallas_call(
        _oproj_kernel,
        grid=(t // tm,),
        in_specs=[pl.BlockSpec((tm, d), lambda i: (i, 0)),
                  pl.BlockSpec((tm, d), lambda i: (i, 0)),
                  pl.BlockSpec((d, d), lambda i: (0, 0)),
                  pl.BlockSpec((1, d), lambda i: (0, 0)),
                  pl.BlockSpec((None, 1, d), lambda i: (i // per_b, 0, 0))],
        out_specs=pl.BlockSpec((tm, d), lambda i: (i, 0)),
        out_shape=jax.ShapeDtypeStruct((t, d), F32),
        compiler_params=_cparams("parallel"),
        name="attn_out_proj",
    )(a, x2, w, g, gate)


def _router_kernel(x_ref, g_ref, sh_ref, sc_ref, wr_ref, h_ref, r_ref):
    h = _norm_mod(x_ref[...], g_ref[...], sh_ref[...], sc_ref[...])
    h_ref[...] = h
    logits = jnp.dot(h, wr_ref[...], preferred_element_type=F32, precision=HIGHEST)
    lane = lax.broadcasted_iota(jnp.int32, logits.shape, 1).astype(F32)
    neg = jnp.float32(-jnp.inf)
    lg = jnp.where(lane < N_EXPERTS, logits, neg)
    m1 = jnp.max(lg, axis=-1, keepdims=True)
    i1 = jnp.min(jnp.where(lg == m1, lane, float(LANES)), axis=-1, keepdims=True)
    lg2 = jnp.where(lane == i1, neg, lg)
    m2 = jnp.max(lg2, axis=-1, keepdims=True)
    i2 = jnp.min(jnp.where(lg2 == m2, lane, float(LANES)), axis=-1, keepdims=True)
    e = jnp.exp(m2 - m1)
    p1 = 1.0 / (1.0 + e)
    p2 = e * p1
    r_ref[...] = jnp.where(lane == 0, i1, jnp.where(lane == 1, i2, jnp.where(lane == 2, p1, jnp.where(lane == 3, p2, 0.0))))


def _router(x2, g, shift, scale, wr_pad, seq, tm=512):
    t, d = x2.shape
    per_b = seq // tm
    mod = pl.BlockSpec((None, 1, d), lambda i: (i // per_b, 0, 0))
    return pl.pallas_call(
        _router_kernel,
        grid=(t // tm,),
        in_specs=[pl.BlockSpec((tm, d), lambda i: (i, 0)),
                  pl.BlockSpec((1, d), lambda i: (0, 0)), mod, mod,
                  pl.BlockSpec((d, LANES), lambda i: (0, 0))],
        out_specs=[pl.BlockSpec((tm, d), lambda i: (i, 0)),
                   pl.BlockSpec((tm, LANES), lambda i: (i, 0))],
        out_shape=[jax.ShapeDtypeStruct((t, d), F32), jax.ShapeDtypeStruct((t, LANES), F32)],
        compiler_params=_cparams("parallel"),
        name="moe_router",
    )(x2, g, shift, scale, wr_pad)


def _row_copy(src_hbm, dst, sem, src_row, dst_row):
    return pltpu.make_async_copy(src_hbm.at[pl.ds(src_row, 1)], dst.at[pl.ds(dst_row, 1)], sem)


def _moe_kernel(te_ref, nu_ref, tok_ref, nxt_ref, h_hbm, wg_ref, wu_ref, wd_ref, o_ref, xbuf, xb16, hid_scr, sem):
    i = pl.program_id(0)
    j = pl.program_id(1)
    tm = xb16.shape[0]
    used = i < nu_ref[0]
    slot = lax.rem(i, 2)

    def gather(idx_ref, s):
        def start(r, c):
            _row_copy(h_hbm, xbuf.at[s], sem.at[s], idx_ref[0, r], r).start()
            return c
        lax.fori_loop(0, tm, start, 0, unroll=8)

    @pl.when(jnp.logical_and(i == 0, j == 0))
    def _():
        gather(tok_ref, 0)

    @pl.when(jnp.logical_and(i + 1 < nu_ref[0], j == 0))
    def _():
        gather(nxt_ref, 1 - slot)

    @pl.when(jnp.logical_and(used, j == 0))
    def _():
        def wait(r, c):
            _row_copy(h_hbm, xbuf.at[slot], sem.at[slot], 0, r).wait()
            return c
        lax.fori_loop(0, tm, wait, 0, unroll=8)
        xb16[...] = xbuf[slot].astype(BF16)

    @pl.when(j == 0)
    def _():
        o_ref[...] = jnp.zeros_like(o_ref)

    @pl.when(used)
    def _():
        o_ref[...] += _swiglu_step(xb16[...], wg_ref, wu_ref, wd_ref, hid_scr)


def _moe_experts(tile_expert, n_used, sorted_tok, h, wg, wu, wd, tm=MOE_TM):
    t, d = h.shape
    f = wd.shape[1]
    nt = sorted_tok.shape[0]
    nf = f // FFN_TF

    def wcol(i, j, te, nu):
        return (te[i], 0, jnp.where(i < nu[0], j, nf - 1))

    def wrow(i, j, te, nu):
        return (te[i], jnp.where(i < nu[0], j, nf - 1), 0)

    grid_spec = pltpu.PrefetchScalarGridSpec(
        num_scalar_prefetch=2,
        grid=(nt, nf),
        in_specs=[pl.BlockSpec((None, 1, tm), lambda i, j, te, nu: (i, 0, 0), memory_space=pltpu.SMEM),
                  pl.BlockSpec((None, 1, tm), lambda i, j, te, nu: (jnp.minimum(i + 1, nt - 1), 0, 0),
                               memory_space=pltpu.SMEM),
                  pl.BlockSpec(memory_space=pl.ANY),
                  pl.BlockSpec((None, d, FFN_TF), wcol),
                  pl.BlockSpec((None, d, FFN_TF), wcol),
                  pl.BlockSpec((None, FFN_TF, d), wrow)],
        out_specs=pl.BlockSpec((tm, d), lambda i, j, te, nu: (i, 0)),
        scratch_shapes=[pltpu.VMEM((2, tm, d), F32), pltpu.VMEM((tm, d), BF16), pltpu.VMEM((tm, FFN_TF), BF16),
                        pltpu.SemaphoreType.DMA((2,))],
    )
    return pl.pallas_call(
        _moe_kernel,
        grid_spec=grid_spec,
        out_shape=jax.ShapeDtypeStruct((nt * tm, d), F32),
        compiler_params=_cparams("arbitrary", "arbitrary"),
        name="moe_experts",
    )(tile_expert, n_used, sorted_tok, sorted_tok, h, wg, wu, wd)


def _combine_kernel(d0_ref, d1_ref, n0_ref, n1_ref, y_hbm, r_ref, x_ref, g_ref, gate_ref, o_ref, buf, sem):
    i = pl.program_id(0)
    tm = x_ref.shape[0]
    slot = lax.rem(i, 2)

    def gather(i0_ref, i1_ref, s):
        def start(r, c):
            _row_copy(y_hbm, buf.at[s, 0], sem.at[s], i0_ref[0, r], r).start()
            _row_copy(y_hbm, buf.at[s, 1], sem.at[s], i1_ref[0, r], r).start()
            return c
        lax.fori_loop(0, tm, start, 0, unroll=8)

    @pl.when(i == 0)
    def _():
        gather(d0_ref, d1_ref, 0)

    @pl.when(i + 1 < pl.num_programs(0))
    def _():
        gather(n0_ref, n1_ref, 1 - slot)

    def wait(r, c):
        _row_copy(y_hbm, buf.at[slot, 0], sem.at[slot], 0, r).wait()
        _row_copy(y_hbm, buf.at[slot, 1], sem.at[slot], 0, r).wait()
        return c

    lax.fori_loop(0, tm, wait, 0, unroll=8)
    r = r_ref[...]
    f = r[:, 2:3] * buf[slot, 0] + r[:, 3:4] * buf[slot, 1]
    o_ref[...] = x_ref[...] + (1.0 + gate_ref[...]) * (_rms(f) * g_ref[...])


def _moe_combine(dest0, dest1, y, route, x2, g, gate, seq, tm=512):
    t, d = x2.shape
    per_b = seq // tm
    nt = t // tm
    idx = pl.BlockSpec((None, 1, tm), lambda i: (i, 0, 0), memory_space=pltpu.SMEM)
    nxt = pl.BlockSpec((None, 1, tm), lambda i: (jnp.minimum(i + 1, nt - 1), 0, 0), memory_space=pltpu.SMEM)
    d0 = dest0.reshape(nt, 1, tm)
    d1 = dest1.reshape(nt, 1, tm)
    return pl.pallas_call(
        _combine_kernel,
        grid=(nt,),
        in_specs=[idx, idx, nxt, nxt,
                  pl.BlockSpec(memory_space=pl.ANY),
                  pl.BlockSpec((tm, LANES), lambda i: (i, 0)),
                  pl.BlockSpec((tm, d), lambda i: (i, 0)),
                  pl.BlockSpec((1, d), lambda i: (0, 0)),
                  pl.BlockSpec((None, 1, d), lambda i: (i // per_b, 0, 0))],
        out_specs=pl.BlockSpec((tm, d), lambda i: (i, 0)),
        out_shape=jax.ShapeDtypeStruct((t, d), F32),
        scratch_shapes=[pltpu.VMEM((2, 2, tm, d), F32), pltpu.SemaphoreType.DMA((2,))],
        compiler_params=_cparams("arbitrary"),
        name="moe_combine",
    )(d0, d1, d0, d1, y, route, x2, g, gate)


def _routing_plan(route, tm):
    t = route.shape[0]
    ex = route[:, 0:2].astype(jnp.int32).T.reshape(-1)
    onehot = (ex[:, None] == jnp.arange(N_EXPERTS, dtype=jnp.int32)[None, :]).astype(jnp.int32)
    csum = jnp.cumsum(onehot, axis=0)
    rank = jnp.sum(onehot * csum, axis=1) - 1
    counts = csum[-1]
    tiles = (counts + tm - 1) // tm
    tile_end = jnp.cumsum(tiles)
    offset = (tile_end - tiles) * tm
    dest = offset[ex] + rank
    nt = (2 * t) // tm + N_EXPERTS
    n_used = tile_end[-1]
    tile_ids = jnp.minimum(jnp.arange(nt, dtype=jnp.int32), n_used - 1)
    tile_expert = jnp.sum((tile_ids[:, None] >= tile_end[None, :]).astype(jnp.int32), axis=1)
    token = jnp.arange(2 * t, dtype=jnp.int32) % t
    sorted_tok = jnp.zeros((nt * tm,), jnp.int32).at[dest].set(token)
    return (tile_expert.astype(jnp.int32), n_used.reshape(1).astype(jnp.int32),
            sorted_tok.reshape(nt, 1, tm), dest[:t], dest[t:])


def kernel(x, c, ada_mix_w, ada_mix_b, norm_mix_pre, norm_mix_post, ada_ffn_w, ada_ffn_b, norm_ffn_pre, norm_ffn_post, s5_a_re, s5_a_im, s5_b_re, s5_b_im, s5_c_re, s5_c_im, s5_d, s5_log_step, s5_glu_w, s5_glu_b, kv_ada_w, kv_ada_b, kv_norm, w_k, w_v, w_q, w_o, ffn_w_gate, ffn_w_up, ffn_w_down, moe_w_router, moe_w_gate, moe_w_up, moe_w_down):
    bsz, seq, d = x.shape
    x2 = x.reshape(bsz * seq, d)

    assert bsz <= SUBLANES
    c_pad = jnp.zeros((SUBLANES, d), F32).at[:bsz].set(c)
    mix = _ada(c_pad, ada_mix_w, ada_mix_b)
    ffn = _ada(c_pad, ada_ffn_w, ada_ffn_b)
    kvm = _ada(c_pad, kv_ada_w[None], kv_ada_b[None])

    def mods(m, layer, n):
        return [m[layer, :bsz, k * d:(k + 1) * d].reshape(bsz, 1, d) for k in range(n)]

    def row(v):
        return v.reshape(1, d)

    shift, scale, gate = mods(mix, 0, 3)
    m1, m2, m3, a1, a2, dsk = _s5_tables(s5_a_re[0], s5_a_im[0], s5_b_re[0], s5_b_im[0],
                                         s5_c_re[0], s5_c_im[0], s5_log_step[0], s5_d[0], bsz)
    ho = _s5_pre(x2, row(norm_mix_pre[0]), shift, scale, seq)
    v = _s5_state(ho, m3, bsz)
    sin = _s5_scan(v, a1, a2, bsz)
    yo = _s5_out(ho, sin, m1, m2, dsk, bsz)
    x2 = _s5_glu(yo, x2, s5_glu_w[0].astype(BF16), row(s5_glu_b[0]), row(norm_mix_post[0]), gate, seq)

    shift, scale, gate = mods(ffn, 0, 3)
    x2 = _ffn(x2, row(norm_ffn_pre[0]), shift, scale, ffn_w_gate[0].astype(BF16), ffn_w_up[0].astype(BF16),
              ffn_w_down[0].astype(BF16), row(norm_ffn_post[0]), gate, seq)

    shift, scale, gate = mods(mix, 1, 3)
    kv_shift, kv_scale = mods(kvm, 0, 2)
    w3 = jnp.stack([w_q[0], w_k, w_v]).astype(BF16)
    qkv = _qkv(x2, row(norm_mix_pre[1]), shift, scale, row(kv_norm), kv_shift, kv_scale, w3, seq)
    att = _attention(qkv, bsz, seq)
    x2 = _oproj(att, x2, w_o[0].astype(BF16), row(norm_mix_post[1]), gate, seq)

    shift, scale, gate = mods(ffn, 1, 3)
    wr_pad = jnp.zeros((d, LANES), F32).at[:, :N_EXPERTS].set(moe_w_router[0])
    h, route = _router(x2, row(norm_ffn_pre[1]), shift, scale, wr_pad, seq)
    tm_moe = MOE_TM
    tile_expert, n_used, sorted_tok, dest0, dest1 = _routing_plan(route, tm_moe)
    y = _moe_experts(tile_expert, n_used, sorted_tok, h, moe_w_gate[0].astype(BF16), moe_w_up[0].astype(BF16),
                     moe_w_down[0].astype(BF16), tm=tm_moe)
    x2 = _moe_combine(dest0, dest1, y, route, x2, row(norm_ffn_post[1]), gate, seq)
    return x2.reshape(bsz, seq, d)
```

```python
import functools
import math

import jax
import jax.numpy as jnp
from jax import lax
from jax.experimental import pallas as pl
from jax.experimental.pallas import tpu as pltpu

F32 = jnp.float32
BF16 = jnp.bfloat16

LANES = 128
SUBLANES = 8
VMEM_LIMIT = 56 * 1024 * 1024
RMS_EPS = 1e-6

S5_GROUP = 16
S5_STATE = 64
CHUNK = 16
OCT = LANES // S5_GROUP
OCT_STATE = 2 * OCT * S5_STATE
N_HEADS = 16
N_EXPERTS = 8
HIGHEST = lax.Precision.HIGHEST


def _cparams(*semantics):
    return pltpu.CompilerParams(dimension_semantics=semantics, vmem_limit_bytes=VMEM_LIMIT)


def _dot(a, b):
    return jnp.dot(a, b, preferred_element_type=F32)


def _rms(xf):
    ms = jnp.mean(xf * xf, axis=-1, keepdims=True)
    return xf * lax.rsqrt(ms + RMS_EPS)


def _norm_mod(xf, g, shift, scale):
    return _rms(xf) * (g * (1.0 + scale)) + shift


def _ada_kernel(c_ref, w_ref, b_ref, o_ref):
    s = c_ref[...]
    s = s * jax.nn.sigmoid(s)
    o_ref[...] = _dot(s.astype(BF16), w_ref[...].astype(BF16)) + b_ref[...]


def _ada(c_pad, w, b, tn=1024):
    nl, d, n = w.shape
    return pl.pallas_call(
        _ada_kernel,
        grid=(nl, n // tn),
        in_specs=[pl.BlockSpec((SUBLANES, d), lambda l, j: (0, 0)),
                  pl.BlockSpec((None, d, tn), lambda l, j: (l, 0, j)),
                  pl.BlockSpec((None, 1, tn), lambda l, j: (l, 0, j))],
        out_specs=pl.BlockSpec((None, SUBLANES, tn), lambda l, j: (l, 0, j)),
        out_shape=jax.ShapeDtypeStruct((nl, SUBLANES, n), F32),
        compiler_params=_cparams("parallel", "parallel"),
        name="ada_mods",
    )(c_pad, w, b.reshape(nl, 1, n))


def _expand_kernel(x_ref, rep_ref, o_ref, *, row_shift, col_shift):
    nb, rows, k = x_ref.shape
    n = o_ref.shape[2]
    x = x_ref[...].reshape(nb * rows, k).astype(BF16)
    spread = _dot(x, rep_ref[...])
    group = lax.shift_right_logical(lax.broadcasted_iota(jnp.int32, (nb * rows, n), 0), row_shift) & (OCT - 1)
    slot = lax.shift_right_logical(lax.broadcasted_iota(jnp.int32, (nb * rows, n), 1), col_shift) & (OCT - 1)
    o_ref[...] = jnp.where(group == slot, spread, 0.0).astype(BF16).reshape(nb, rows, n)


def _group_expand(x, rows_per_group, inner, nb):
    b, rows, k = x.shape
    n = k * OCT
    kk = jnp.arange(k, dtype=jnp.int32)
    nn = jnp.arange(n, dtype=jnp.int32)
    rep = jnp.logical_and((kk // inner)[:, None] == (nn // (inner * OCT))[None, :],
                          (kk % inner)[:, None] == (nn % inner)[None, :]).astype(BF16)
    return pl.pallas_call(
        functools.partial(_expand_kernel, row_shift=rows_per_group.bit_length() - 1,
                          col_shift=inner.bit_length() - 1),
        grid=(b // nb,),
        in_specs=[pl.BlockSpec((nb, rows, k), lambda i: (i, 0, 0)),
                  pl.BlockSpec((k, n), lambda i: (0, 0))],
        out_specs=pl.BlockSpec((nb, rows, n), lambda i: (i, 0, 0)),
        out_shape=jax.ShapeDtypeStruct((b, rows, n), BF16),
        compiler_params=_cparams("parallel"),
        name="s5_table_expand",
    )(x, rep)


def _s5_tables(a_re, a_im, b_re, b_im, c_re, c_im, log_step, d_skip, batch):
    g, p = a_re.shape
    h, lc = S5_GROUP, CHUNK
    no = g // OCT
    dt = jnp.exp(log_step)[:, None]
    lr, li = a_re * dt, a_im * dt
    k = jnp.arange(lc + 1, dtype=F32)[:, None, None]
    mag = jnp.exp(k * lr[None])
    pw_r, pw_i = mag * jnp.cos(k * li[None]), mag * jnp.sin(k * li[None])
    nr, ni = pw_r[1] - 1.0, pw_i[1]
    den = a_re * a_re + a_im * a_im
    qr, qi = (nr * a_re + ni * a_im) / den, (ni * a_re - nr * a_im) / den
    bbr = qr[..., None] * b_re - qi[..., None] * b_im
    bbi = qr[..., None] * b_im + qi[..., None] * b_re
    same_group = jnp.eye(OCT, dtype=F32).reshape(1, 1, OCT, 1, 1, OCT, 1)

    wr, wi = pw_r[:lc, :, None, :], pw_i[:lc, :, None, :]
    cwr = c_re[None] * wr - c_im[None] * wi
    cwi = c_re[None] * wi + c_im[None] * wr
    kk = jnp.einsum('tgop,gpi->tgio', jnp.concatenate([cwr, -cwi], axis=-1),
                    jnp.concatenate([bbr, bbi], axis=1), precision=HIGHEST)
    kk = kk.reshape(lc, no, OCT, h, h).transpose(1, 0, 2, 3, 4)
    kd = (kk[:, :, :, :, None, None, :] * same_group).reshape(no, lc, LANES, LANES)
    kd = jnp.concatenate([jnp.zeros((no, 1, LANES, LANES), F32), kd], axis=1)
    tiles = []
    for delta in range(lc // 2 - 1, -1, -1):
        top = jnp.concatenate([kd[:, 2 * delta + 1], kd[:, 2 * delta + 2]], axis=-1)
        bottom = jnp.concatenate([kd[:, 2 * delta], kd[:, 2 * delta + 1]], axis=-1)
        tiles += [top, bottom]
    m1 = jnp.concatenate(tiles, axis=-2)

    er = c_re[None] * pw_r[1:, :, None, :] - c_im[None] * pw_i[1:, :, None, :]
    ei = c_re[None] * pw_i[1:, :, None, :] + c_im[None] * pw_r[1:, :, None, :]
    e2 = jnp.stack([er, -ei]).reshape(2, lc, no, OCT, h, p).transpose(2, 0, 3, 5, 1, 4)
    m2 = _group_expand(e2.reshape(no * 2, OCT * p, lc * h), p, h, nb=1).reshape(no, OCT_STATE, lc * LANES)

    fr = pw_r[lc - 1::-1][:lc, :, :, None] * bbr[None] - pw_i[lc - 1::-1][:lc, :, :, None] * bbi[None]
    fi = pw_r[lc - 1::-1][:lc, :, :, None] * bbi[None] + pw_i[lc - 1::-1][:lc, :, :, None] * bbr[None]
    f2 = jnp.stack([fr, fi]).reshape(2, lc, no, OCT, p, h).transpose(2, 1, 3, 5, 0, 4)
    m3 = _group_expand(f2.reshape(no * lc, OCT * h, 2 * p), h, p, nb=8).reshape(no, lc * LANES, OCT_STATE)

    ar = pw_r[lc].reshape(no, OCT * p)
    ai = pw_i[lc].reshape(no, OCT * p)
    a1 = jnp.tile(jnp.concatenate([ar, ar], axis=-1), (1, batch))[:, None, :]
    a2 = jnp.tile(jnp.concatenate([-ai, ai], axis=-1), (1, batch))[:, None, :]
    dsk = jnp.tile(d_skip.reshape(no, 1, LANES), (1, 1, lc))
    return m1.astype(BF16), m2.astype(BF16), m3.astype(BF16), a1, a2, dsk


def _s5_pre_kernel(x_ref, g_ref, sh_ref, sc_ref, o_ref, slab):
    tm = x_ref.shape[0]
    nt = x_ref.shape[1] // LANES
    rows = tm // CHUNK
    h = _norm_mod(x_ref[...], g_ref[...], sh_ref[...], sc_ref[...])
    for j in range(nt):
        slab[j] = h[:, j * LANES:(j + 1) * LANES]
    for j in range(nt):
        for t in range(CHUNK):
            o_ref[j, :, t * LANES:(t + 1) * LANES] = slab.at[j][pl.ds(t, rows, stride=CHUNK), :].astype(BF16)


def _s5_pre(x2, g, shift, scale, seq, tm=512):
    t, d = x2.shape
    nt = d // LANES
    per_b = seq // tm
    return pl.pallas_call(
        _s5_pre_kernel,
        grid=(t // tm,),
        in_specs=[pl.BlockSpec((tm, d), lambda i: (i, 0)),
                  pl.BlockSpec((1, d), lambda i: (0, 0)),
                  pl.BlockSpec((None, 1, d), lambda i: (i // per_b, 0, 0)),
                  pl.BlockSpec((None, 1, d), lambda i: (i // per_b, 0, 0))],
        out_specs=pl.BlockSpec((nt, tm // CHUNK, CHUNK * LANES), lambda i: (0, i, 0)),
        out_shape=jax.ShapeDtypeStruct((nt, t // CHUNK, CHUNK * LANES), BF16),
        scratch_shapes=[pltpu.VMEM((nt, tm, LANES), F32)],
        compiler_params=_cparams("parallel"),
        name="s5_pre",
    )(x2, g, shift, scale)


def _s5_state_kernel(h_ref, m3_ref, o_ref):
    o_ref[...] = _dot(h_ref[...], m3_ref[...])


def _s5_state(ho, m3, batch):
    no, r, kc = ho.shape
    c = r // batch
    return pl.pallas_call(
        _s5_state_kernel,
        grid=(no, batch),
        in_specs=[pl.BlockSpec((None, c, kc), lambda j, b: (j, b, 0)),
                  pl.BlockSpec((None, kc, OCT_STATE), lambda j, b: (j, 0, 0))],
        out_specs=pl.BlockSpec((None, c, OCT_STATE), lambda j, b: (j, 0, b)),
        out_shape=jax.ShapeDtypeStruct((no, c, batch * OCT_STATE), F32),
        compiler_params=_cparams("parallel", "parallel"),
        name="s5_state",
    )(ho, m3)


def _s5_scan_kernel(v_ref, a1_ref, a2_ref, o_ref, *, batch, unroll):
    a1 = a1_ref[...]
    a2 = a2_ref[...]
    half = OCT_STATE // 2

    def swap(s):
        parts = []
        for b in range(batch):
            parts.append(s[:, b * OCT_STATE + half:(b + 1) * OCT_STATE])
            parts.append(s[:, b * OCT_STATE:b * OCT_STATE + half])
        return jnp.concatenate(parts, axis=-1)

    def body(i, s):
        base = pl.multiple_of(i * unroll, unroll)
        v = v_ref[pl.ds(base, unroll), :]
        rows = []
        for k in range(unroll):
            rows.append(s)
            s = a1 * s + a2 * swap(s) + v[k:k + 1, :]
        o_ref[pl.ds(base, unroll), :] = jnp.concatenate(rows, axis=0).astype(o_ref.dtype)
        return s

    lax.fori_loop(0, v_ref.shape[0] // unroll, body, jnp.zeros((1, v_ref.shape[1]), F32))


def _s5_scan(v, a1, a2, batch, unroll=16):
    no, c, w = v.shape
    return pl.pallas_call(
        functools.partial(_s5_scan_kernel, batch=batch, unroll=unroll),
        grid=(no,),
        in_specs=[pl.BlockSpec((None, c, w), lambda j: (j, 0, 0)),
                  pl.BlockSpec((None, 1, w), lambda j: (j, 0, 0)),
                  pl.BlockSpec((None, 1, w), lambda j: (j, 0, 0))],
        out_specs=pl.BlockSpec((None, c, w), lambda j: (j, 0, 0)),
        out_shape=jax.ShapeDtypeStruct((no, c, w), BF16),
        compiler_params=_cparams("parallel"),
        name="s5_scan",
    )(v, a1, a2)


def _s5_out_kernel(h_ref, s_ref, m1_ref, m2_ref, d_ref, o_ref):
    tw = m1_ref.shape[1]
    n_tiles = h_ref.shape[1] // tw
    for tile in range(n_tiles):
        cols = slice(tile * tw, (tile + 1) * tw)
        y = (_dot(h_ref[:, :(tile + 1) * tw], m1_ref[(n_tiles - 1 - tile) * tw:, :])
             + _dot(s_ref[...], m2_ref[:, cols])
             + d_ref[:, cols] * h_ref[:, cols].astype(F32))
        o_ref[:, cols] = jax.nn.gelu(y).astype(BF16)


def _s5_out(ho, sin, m1, m2, dsk, batch):
    no, r, kc = ho.shape
    c = r // batch
    return pl.pallas_call(
        _s5_out_kernel,
        grid=(no, batch),
        in_specs=[pl.BlockSpec((None, c, kc), lambda j, b: (j, b, 0)),
                  pl.BlockSpec((None, c, OCT_STATE), lambda j, b: (j, 0, b)),
                  pl.BlockSpec((None, kc, m1.shape[2]), lambda j, b: (j, 0, 0)),
                  pl.BlockSpec((None, OCT_STATE, kc), lambda j, b: (j, 0, 0)),
                  pl.BlockSpec((None, 1, kc), lambda j, b: (j, 0, 0))],
        out_specs=pl.BlockSpec((None, c, kc), lambda j, b: (j, b, 0)),
        out_shape=jax.ShapeDtypeStruct((no, r, kc), BF16),
        compiler_params=_cparams("parallel", "parallel"),
        name="s5_out",
    )(ho, sin, m1, m2, dsk)


def _s5_glu_kernel(y_ref, x_ref, w_ref, b_ref, g_ref, gate_ref, o_ref, ybuf, slab):
    nt, rows, _ = y_ref.shape
    for t in range(CHUNK):
        for j in range(nt):
            ybuf[t * rows:(t + 1) * rows, j * LANES:(j + 1) * LANES] = y_ref[j, :, t * LANES:(t + 1) * LANES]
    y = ybuf[...]
    z = _dot(y, w_ref[...]) + b_ref[...]
    m = y.astype(F32) * jax.nn.sigmoid(z)
    res = (1.0 + gate_ref[...]) * (_rms(m) * g_ref[...])
    for j in range(nt):
        for t in range(CHUNK):
            slab.at[j][pl.ds(t, rows, stride=CHUNK), :] = res[t * rows:(t + 1) * rows, j * LANES:(j + 1) * LANES]
    for j in range(nt):
        o_ref[:, j * LANES:(j + 1) * LANES] = x_ref[:, j * LANES:(j + 1) * LANES] + slab[j]


def _s5_glu(yo, x2, w, b, g, gate, seq, tm=512):
    t, d = x2.shape
    nt = d // LANES
    per_b = seq // tm
    return pl.pallas_call(
        _s5_glu_kernel,
        grid=(t // tm,),
        in_specs=[pl.BlockSpec((nt, tm // CHUNK, CHUNK * LANES), lambda i: (0, i, 0)),
                  pl.BlockSpec((tm, d), lambda i: (i, 0)),
                  pl.BlockSpec((d, d), lambda i: (0, 0)),
                  pl.BlockSpec((1, d), lambda i: (0, 0)),
                  pl.BlockSpec((1, d), lambda i: (0, 0)),
                  pl.BlockSpec((None, 1, d), lambda i: (i // per_b, 0, 0))],
        out_specs=pl.BlockSpec((tm, d), lambda i: (i, 0)),
        out_shape=jax.ShapeDtypeStruct((t, d), F32),
        scratch_shapes=[pltpu.VMEM((tm, d), BF16), pltpu.VMEM((nt, tm, LANES), F32)],
        compiler_params=_cparams("parallel"),
        name="s5_glu",
    )(yo, x2, w, b, g, gate)


FFN_TF = 1024
MOE_TM = 512


def _swiglu_step(h, wg_ref, wu_ref, wd_ref, hid_scr):
    for c in range(hid_scr.shape[1] // LANES):
        cols = slice(c * LANES, (c + 1) * LANES)
        r = _dot(h, jnp.concatenate([wg_ref[:, cols], wu_ref[:, cols]], axis=1))
        a, u = r[:, :LANES], r[:, LANES:]
        hid_scr[:, cols] = (a * jax.nn.sigmoid(a) * u).astype(BF16)
    return _dot(hid_scr[...], wd_ref[...])


def _ffn_kernel(x_ref, g_ref, sh_ref, sc_ref, wg_ref, wu_ref, wd_ref, gp_ref, gate_ref, o_ref, h_scr, hid_scr):
    j = pl.program_id(1)

    @pl.when(j == 0)
    def _():
        h_scr[...] = _norm_mod(x_ref[...], g_ref[...], sh_ref[...], sc_ref[...]).astype(BF16)
        o_ref[...] = jnp.zeros_like(o_ref)

    o_ref[...] += _swiglu_step(h_scr[...], wg_ref, wu_ref, wd_ref, hid_scr)

    @pl.when(j == pl.num_programs(1) - 1)
    def _():
        o_ref[...] = x_ref[...] + (1.0 + gate_ref[...]) * (_rms(o_ref[...]) * gp_ref[...])


def _ffn(x2, g, shift, scale, wg, wu, wd, gp, gate, seq, tm=512):
    t, d = x2.shape
    f = wd.shape[0]
    per_b = seq // tm
    mod = pl.BlockSpec((None, 1, d), lambda i, j: (i // per_b, 0, 0))
    row = pl.BlockSpec((1, d), lambda i, j: (0, 0))
    wcol = pl.BlockSpec((d, FFN_TF), lambda i, j: (0, j))
    return pl.pallas_call(
        _ffn_kernel,
        grid=(t // tm, f // FFN_TF),
        in_specs=[pl.BlockSpec((tm, d), lambda i, j: (i, 0)), row, mod, mod, wcol, wcol,
                  pl.BlockSpec((FFN_TF, d), lambda i, j: (j, 0)),
                  row, mod],
        out_specs=pl.BlockSpec((tm, d), lambda i, j: (i, 0)),
        out_shape=jax.ShapeDtypeStruct((t, d), F32),
        scratch_shapes=[pltpu.VMEM((tm, d), BF16), pltpu.VMEM((tm, FFN_TF), BF16)],
        compiler_params=_cparams("parallel", "arbitrary"),
        name="dense_ffn",
    )(x2, g, shift, scale, wg, wu, wd, gp, gate)


def _qkv_kernel(x_ref, gq_ref, shq_ref, scq_ref, gkv_ref, shkv_ref, sckv_ref, w_ref, o_ref, hq, hkv, *, qscale):
    n = pl.program_id(1)

    @pl.when(n == 0)
    def _():
        xf = x_ref[...]
        xn = _rms(xf)
        hq[...] = (xn * (gq_ref[...] * (1.0 + scq_ref[...])) + shq_ref[...]).astype(BF16)
        hkv[...] = (xn * (gkv_ref[...] * (1.0 + sckv_ref[...])) + shkv_ref[...]).astype(BF16)
        o_ref[...] = (_dot(hq[...], w_ref[...]) * qscale).astype(BF16)

    @pl.when(n > 0)
    def _():
        o_ref[...] = _dot(hkv[...], w_ref[...]).astype(BF16)


def _qkv(x2, gq, shq, scq, gkv, shkv, sckv, w3, seq, tm=512):
    t, d = x2.shape
    per_b = seq // tm
    mod = pl.BlockSpec((None, 1, d), lambda i, n: (i // per_b, 0, 0))
    row = pl.BlockSpec((1, d), lambda i, n: (0, 0))
    qscale = math.log2(math.e) / math.sqrt(d // N_HEADS)
    return pl.pallas_call(
        functools.partial(_qkv_kernel, qscale=qscale),
        grid=(t // tm, 3),
        in_specs=[pl.BlockSpec((tm, d), lambda i, n: (i, 0)), row, mod, mod, row, mod, mod,
                  pl.BlockSpec((None, d, d), lambda i, n: (n, 0, 0))],
        out_specs=pl.BlockSpec((None, tm, d), lambda i, n: (n, i, 0)),
        out_shape=jax.ShapeDtypeStruct((3, t, d), BF16),
        scratch_shapes=[pltpu.VMEM((tm, d), BF16), pltpu.VMEM((tm, d), BF16)],
        compiler_params=_cparams("parallel", "arbitrary"),
        name="qkv_proj",
    )(x2, gq, shq, scq, gkv, shkv, sckv, w3)


ATTN_TK = 256
EXP2_ZERO_BELOW = -150.0
NO_WEIGHT = 1e30


def _attn_kernel(q_ref, k_ref, v_ref, o_ref, *, tb, tk, nh, hd):
    qi = pl.program_id(2)
    causal = (lax.broadcasted_iota(jnp.int32, (tb, tk), 1)
              < lax.broadcasted_iota(jnp.int32, (tb, tk), 0))
    krow = lax.broadcasted_iota(jnp.int32, (tk, tk), 0)
    kcol = lax.broadcasted_iota(jnp.int32, (tk, tk), 1)
    later = jnp.where(krow > kcol, 1.0, 0.0).astype(BF16)
    later2 = jnp.concatenate([later, later], axis=0)
    sign_bit = jnp.uint32(0x80000000)

    heads = range(nh)
    lanes = [slice(g * hd, (g + 1) * hd) for g in heads]

    def walk(spans, accs, runs):
        pairs = [(b, g) for b in range(len(spans)) for g in heads]
        start = [pl.multiple_of(s, tk) for s, _, _ in spans]
        y, logb, sm, suffix = {}, {}, {}, {}
        for b, g in pairs:
            y[b, g] = lax.dot_general(q_ref[:, lanes[g]], k_ref[pl.ds(start[b], tk), lanes[g]],
                                      (((1,), (1,)), ((), ())), preferred_element_type=F32)
        for b, g in pairs:
            neg_abs = lax.bitcast_convert_type(lax.bitcast_convert_type(y[b, g], jnp.uint32) | sign_bit, F32)
            s = jnp.maximum(y[b, g], 0.0) + jnp.log2(1.0 + jnp.exp2(neg_abs))
            logb[b, g] = y[b, g] - s
            sm[b, g] = jnp.where(causal, s, 0.0) if spans[b][1] else s
        for b, g in pairs:
            hi = sm[b, g].astype(BF16)
            lo = (sm[b, g] - hi.astype(F32)).astype(BF16)
            suffix[b, g] = _dot(jnp.concatenate([hi, lo], axis=1), later2)
        new_accs, new_runs = [], []
        for g in heads:
            run, acc = runs[g], accs[g]
            for b, (_, diagonal, exists) in enumerate(spans):
                if exists is not True:
                    run = jnp.where(exists, run, NO_WEIGHT)
                w = jnp.exp2((logb[b, g] - suffix[b, g]) - run)
                if diagonal:
                    w = jnp.where(causal, w, 0.0)
                acc = acc + _dot(w.astype(BF16), v_ref[pl.ds(start[b], tk), lanes[g]])
                run = run + suffix[b, g][:, 0:1] + sm[b, g][:, 0:1]
            new_accs.append(acc)
            new_runs.append(run)
        return tuple(new_accs), tuple(new_runs)

    accs = tuple(jnp.zeros((tb, hd), F32) for _ in heads)
    runs = tuple(jnp.zeros((tb, 1), F32) for _ in heads)
    start_a = qi * tb
    start_b = jnp.maximum(start_a - tk, 0)
    accs, runs = walk([(start_a, True, True), (start_b, False, qi > 0)], accs, runs)

    def cond(c):
        left, _, runs = c
        low = runs[0]
        for g in range(1, nh):
            low = jnp.minimum(low, runs[g])
        return jnp.logical_and(left > 0, jnp.min(low) <= -EXP2_ZERO_BELOW)

    def body(c):
        left, accs, runs = c
        start = left - tk
        accs, runs = walk([(start, False, True)], accs, runs)
        return start, accs, runs

    _, accs, _ = lax.while_loop(cond, body, (start_b, accs, runs))
    for g in range(nh):
        o_ref[:, g * hd:(g + 1) * hd] = accs[g].astype(o_ref.dtype)


def _attention(qkv, batch, seq, tb=256, tk=ATTN_TK, nh=4):
    _, t, d = qkv.shape
    assert tb == tk, "query tiles and key spans must have the same size"
    hd = d // N_HEADS
    nq = seq // tb
    return pl.pallas_call(
        functools.partial(_attn_kernel, tb=tb, tk=tk, nh=nh, hd=hd),
        grid=(batch, N_HEADS // nh, nq),
        in_specs=[pl.BlockSpec((None, tb, nh * hd), lambda b, h, i: (0, b * nq + i, h)),
                  pl.BlockSpec((None, seq, nh * hd), lambda b, h, i: (1, b, h)),
                  pl.BlockSpec((None, seq, nh * hd), lambda b, h, i: (2, b, h))],
        out_specs=pl.BlockSpec((tb, nh * hd), lambda b, h, i: (b * nq + i, h)),
        out_shape=jax.ShapeDtypeStruct((t, d), BF16),
        compiler_params=_cparams("parallel", "parallel", "parallel"),
        name="stickbreak_attn",
    )(qkv, qkv, qkv)


def _oproj_kernel(a_ref, x_ref, w_ref, g_ref, gate_ref, o_ref):
    m = _dot(a_ref[...], w_ref[...])
    o_ref[...] = x_ref[...] + (1.0 + gate_ref[...]) * (_rms(m) * g_ref[...])


def _oproj(a, x2, w, g, gate, seq, tm=512):
    t, d = x2.shape
    per_b = seq // tm
    return pl.pallas_call(
        _oproj_kernel,
        grid=(t // tm,),
        in_specs=[pl.BlockSpec((tm, d), lambda i: (i, 0)),
                  pl.BlockSpec((tm, d), lambda i: (i, 0)),
                  pl.BlockSpec((d, d), lambda i: (0, 0)),
                  pl.BlockSpec((1, d), lambda i: (0, 0)),
                  pl.BlockSpec((None, 1, d), lambda i: (i // per_b, 0, 0))],
        out_specs=pl.BlockSpec((tm, d), lambda i: (i, 0)),
        out_shape=jax.ShapeDtypeStruct((t, d), F32),
        compiler_params=_cparams("parallel"),
        name="attn_out_proj",
    )(a, x2, w, g, gate)


def _router_kernel(x_ref, g_ref, sh_ref, sc_ref, wr_ref, h_ref, r_ref):
    h = _norm_mod(x_ref[...], g_ref[...], sh_ref[...], sc_ref[...])
    h_ref[...] = h
    logits = jnp.dot(h, wr_ref[...], preferred_element_type=F32, precision=HIGHEST)
    lane = lax.broadcasted_iota(jnp.int32, logits.shape, 1).astype(F32)
    neg = jnp.float32(-jnp.inf)
    lg = jnp.where(lane < N_EXPERTS, logits, neg)
    m1 = jnp.max(lg, axis=-1, keepdims=True)
    i1 = jnp.min(jnp.where(lg == m1, lane, float(LANES)), axis=-1, keepdims=True)
    lg2 = jnp.where(lane == i1, neg, lg)
    m2 = jnp.max(lg2, axis=-1, keepdims=True)
    i2 = jnp.min(jnp.where(lg2 == m2, lane, float(LANES)), axis=-1, keepdims=True)
    e = jnp.exp(m2 - m1)
    p1 = 1.0 / (1.0 + e)
    p2 = e * p1
    r_ref[...] = jnp.where(lane == 0, i1, jnp.where(lane == 1, i2, jnp.where(lane == 2, p1, jnp.where(lane == 3, p2, 0.0))))


def _router(x2, g, shift, scale, wr_pad, seq, tm=512):
    t, d = x2.shape
    per_b = seq // tm
    mod = pl.BlockSpec((None, 1, d), lambda i: (i // per_b, 0, 0))
    return pl.pallas_call(
        _router_kernel,
        grid=(t // tm,),
        in_specs=[pl.BlockSpec((tm, d), lambda i: (i, 0)),
                  pl.BlockSpec((1, d), lambda i: (0, 0)), mod, mod,
                  pl.BlockSpec((d, LANES), lambda i: (0, 0))],
        out_specs=[pl.BlockSpec((tm, d), lambda i: (i, 0)),
                   pl.BlockSpec((tm, LANES), lambda i: (i, 0))],
        out_shape=[jax.ShapeDtypeStruct((t, d), F32), jax.ShapeDtypeStruct((t, LANES), F32)],
        compiler_params=_cparams("parallel"),
        name="moe_router",
    )(x2, g, shift, scale, wr_pad)


def _row_copy(src_hbm, dst, sem, src_row, dst_row):
    return pltpu.make_async_copy(src_hbm.at[pl.ds(src_row, 1)], dst.at[pl.ds(dst_row, 1)], sem)


def _moe_kernel(te_ref, nu_ref, tok_ref, nxt_ref, h_hbm, wg_ref, wu_ref, wd_ref, o_ref, xbuf, xb16, hid_scr, sem):
    i = pl.program_id(0)
    j = pl.program_id(1)
    tm = xb16.shape[0]
    used = i < nu_ref[0]
    slot = lax.rem(i, 2)

    def gather(idx_ref, s):
        def start(r, c):
            _row_copy(h_hbm, xbuf.at[s], sem.at[s], idx_ref[0, r], r).start()
            return c
        lax.fori_loop(0, tm, start, 0, unroll=8)

    @pl.when(jnp.logical_and(i == 0, j == 0))
    def _():
        gather(tok_ref, 0)

    @pl.when(jnp.logical_and(i + 1 < nu_ref[0], j == 0))
    def _():
        gather(nxt_ref, 1 - slot)

    @pl.when(jnp.logical_and(used, j == 0))
    def _():
        def wait(r, c):
            _row_copy(h_hbm, xbuf.at[slot], sem.at[slot], 0, r).wait()
            return c
        lax.fori_loop(0, tm, wait, 0, unroll=8)
        xb16[...] = xbuf[slot].astype(BF16)

    @pl.when(j == 0)
    def _():
        o_ref[...] = jnp.zeros_like(o_ref)

    @pl.when(used)
    def _():
        o_ref[...] += _swiglu_step(xb16[...], wg_ref, wu_ref, wd_ref, hid_scr)


def _moe_experts(tile_expert, n_used, sorted_tok, h, wg, wu, wd, tm=MOE_TM):
    t, d = h.shape
    f = wd.shape[1]
    nt = sorted_tok.shape[0]
    nf = f // FFN_TF

    def wcol(i, j, te, nu):
        return (te[i], 0, jnp.where(i < nu[0], j, nf - 1))

    def wrow(i, j, te, nu):
        return (te[i], jnp.where(i < nu[0], j, nf - 1), 0)

    grid_spec = pltpu.PrefetchScalarGridSpec(
        num_scalar_prefetch=2,
        grid=(nt, nf),
        in_specs=[pl.BlockSpec((None, 1, tm), lambda i, j, te, nu: (i, 0, 0), memory_space=pltpu.SMEM),
                  pl.BlockSpec((None, 1, tm), lambda i, j, te, nu: (jnp.minimum(i + 1, nt - 1), 0, 0),
                               memory_space=pltpu.SMEM),
                  pl.BlockSpec(memory_space=pl.ANY),
                  pl.BlockSpec((None, d, FFN_TF), wcol),
                  pl.BlockSpec((None, d, FFN_TF), wcol),
                  pl.BlockSpec((None, FFN_TF, d), wrow)],
        out_specs=pl.BlockSpec((tm, d), lambda i, j, te, nu: (i, 0)),
        scratch_shapes=[pltpu.VMEM((2, tm, d), F32), pltpu.VMEM((tm, d), BF16), pltpu.VMEM((tm, FFN_TF), BF16),
                        pltpu.SemaphoreType.DMA((2,))],
    )
    return pl.pallas_call(
        _moe_kernel,
        grid_spec=grid_spec,
        out_shape=jax.ShapeDtypeStruct((nt * tm, d), F32),
        compiler_params=_cparams("arbitrary", "arbitrary"),
        name="moe_experts",
    )(tile_expert, n_used, sorted_tok, sorted_tok, h, wg, wu, wd)


def _combine_kernel(d0_ref, d1_ref, n0_ref, n1_ref, y_hbm, r_ref, x_ref, g_ref, gate_ref, o_ref, buf, sem):
    i = pl.program_id(0)
    tm = x_ref.shape[0]
    slot = lax.rem(i, 2)

    def gather(i0_ref, i1_ref, s):
        def start(r, c):
            _row_copy(y_hbm, buf.at[s, 0], sem.at[s], i0_ref[0, r], r).start()
            _row_copy(y_hbm, buf.at[s, 1], sem.at[s], i1_ref[0, r], r).start()
            return c
        lax.fori_loop(0, tm, start, 0, unroll=8)

    @pl.when(i == 0)
    def _():
        gather(d0_ref, d1_ref, 0)

    @pl.when(i + 1 < pl.num_programs(0))
    def _():
        gather(n0_ref, n1_ref, 1 - slot)

    def wait(r, c):
        _row_copy(y_hbm, buf.at[slot, 0], sem.at[slot], 0, r).wait()
        _row_copy(y_hbm, buf.at[slot, 1], sem.at[slot], 0, r).wait()
        return c

    lax.fori_loop(0, tm, wait, 0, unroll=8)
    r = r_ref[...]
    f = r[:, 2:3] * buf[slot, 0] + r[:, 3:4] * buf[slot, 1]
    o_ref[...] = x_ref[...] + (1.0 + gate_ref[...]) * (_rms(f) * g_ref[...])


def _moe_combine(dest0, dest1, y, route, x2, g, gate, seq, tm=256):
    t, d = x2.shape
    per_b = seq // tm
    nt = t // tm
    idx = pl.BlockSpec((None, 1, tm), lambda i: (i, 0, 0), memory_space=pltpu.SMEM)
    nxt = pl.BlockSpec((None, 1, tm), lambda i: (jnp.minimum(i + 1, nt - 1), 0, 0), memory_space=pltpu.SMEM)
    d0 = dest0.reshape(nt, 1, tm)
    d1 = dest1.reshape(nt, 1, tm)
    return pl.pallas_call(
        _combine_kernel,
        grid=(nt,),
        in_specs=[idx, idx, nxt, nxt,
                  pl.BlockSpec(memory_space=pl.ANY),
                  pl.BlockSpec((tm, LANES), lambda i: (i, 0)),
                  pl.BlockSpec((tm, d), lambda i: (i, 0)),
                  pl.BlockSpec((1, d), lambda i: (0, 0)),
                  pl.BlockSpec((None, 1, d), lambda i: (i // per_b, 0, 0))],
        out_specs=pl.BlockSpec((tm, d), lambda i: (i, 0)),
        out_shape=jax.ShapeDtypeStruct((t, d), F32),
        scratch_shapes=[pltpu.VMEM((2, 2, tm, d), F32), pltpu.SemaphoreType.DMA((2,))],
        compiler_params=_cparams("arbitrary"),
        name="moe_combine",
    )(d0, d1, d0, d1, y, route, x2, g, gate)


def _routing_plan(route, tm):
    t = route.shape[0]
    ex = route[:, 0:2].astype(jnp.int32).T.reshape(-1)
    onehot = (ex[:, None] == jnp.arange(N_EXPERTS, dtype=jnp.int32)[None, :]).astype(jnp.int32)
    csum = jnp.cumsum(onehot, axis=0)
    rank = jnp.sum(onehot * csum, axis=1) - 1
    counts = csum[-1]
    tiles = (counts + tm - 1) // tm
    tile_end = jnp.cumsum(tiles)
    offset = (tile_end - tiles) * tm
    dest = offset[ex] + rank
    nt = (2 * t) // tm + N_EXPERTS
    n_used = tile_end[-1]
    tile_ids = jnp.minimum(jnp.arange(nt, dtype=jnp.int32), n_used - 1)
    tile_expert = jnp.sum((tile_ids[:, None] >= tile_end[None, :]).astype(jnp.int32), axis=1)
    token = jnp.arange(2 * t, dtype=jnp.int32) % t
    sorted_tok = jnp.zeros((nt * tm,), jnp.int32).at[dest].set(token)
    return (tile_expert.astype(jnp.int32), n_used.reshape(1).astype(jnp.int32),
            sorted_tok.reshape(nt, 1, tm), dest[:t], dest[t:])


def kernel(x, c, ada_mix_w, ada_mix_b, norm_mix_pre, norm_mix_post, ada_ffn_w, ada_ffn_b, norm_ffn_pre, norm_ffn_post, s5_a_re, s5_a_im, s5_b_re, s5_b_im, s5_c_re, s5_c_im, s5_d, s5_log_step, s5_glu_w, s5_glu_b, kv_ada_w, kv_ada_b, kv_norm, w_k, w_v, w_q, w_o, ffn_w_gate, ffn_w_up, ffn_w_down, moe_w_router, moe_w_gate, moe_w_up, moe_w_down):
    bsz, seq, d = x.shape
    x2 = x.reshape(bsz * seq, d)

    assert bsz <= SUBLANES
    c_pad = jnp.zeros((SUBLANES, d), F32).at[:bsz].set(c)
    mix = _ada(c_pad, ada_mix_w, ada_mix_b)
    ffn = _ada(c_pad, ada_ffn_w, ada_ffn_b)
    kvm = _ada(c_pad, kv_ada_w[None], kv_ada_b[None])

    def mods(m, layer, n):
        return [m[layer, :bsz, k * d:(k + 1) * d].reshape(bsz, 1, d) for k in range(n)]

    def row(v):
        return v.reshape(1, d)

    shift, scale, gate = mods(mix, 0, 3)
    m1, m2, m3, a1, a2, dsk = _s5_tables(s5_a_re[0], s5_a_im[0], s5_b_re[0], s5_b_im[0],
                                         s5_c_re[0], s5_c_im[0], s5_log_step[0], s5_d[0], bsz)
    ho = _s5_pre(x2, row(norm_mix_pre[0]), shift, scale, seq)
    v = _s5_state(ho, m3, bsz)
    sin = _s5_scan(v, a1, a2, bsz)
    yo = _s5_out(ho, sin, m1, m2, dsk, bsz)
    x2 = _s5_glu(yo, x2, s5_glu_w[0].astype(BF16), row(s5_glu_b[0]), row(norm_mix_post[0]), gate, seq)

    shift, scale, gate = mods(ffn, 0, 3)
    x2 = _ffn(x2, row(norm_ffn_pre[0]), shift, scale, ffn_w_gate[0].astype(BF16), ffn_w_up[0].astype(BF16),
              ffn_w_down[0].astype(BF16), row(norm_ffn_post[0]), gate, seq)

    shift, scale, gate = mods(mix, 1, 3)
    kv_shift, kv_scale = mods(kvm, 0, 2)
    w3 = jnp.stack([w_q[0], w_k, w_v]).astype(BF16)
    qkv = _qkv(x2, row(norm_mix_pre[1]), shift, scale, row(kv_norm), kv_shift, kv_scale, w3, seq)
    att = _attention(qkv, bsz, seq)
    x2 = _oproj(att, x2, w_o[0].astype(BF16), row(norm_mix_post[1]), gate, seq)

    shift, scale, gate = mods(ffn, 1, 3)
    wr_pad = jnp.zeros((d, LANES), F32).at[:, :N_EXPERTS].set(moe_w_router[0])
    h, route = _router(x2, row(norm_ffn_pre[1]), shift, scale, wr_pad, seq)
    tm_moe = MOE_TM
    tile_expert, n_used, sorted_tok, dest0, dest1 = _routing_plan(route, tm_moe)
    y = _moe_experts(tile_expert, n_used, sorted_tok, h, moe_w_gate[0].astype(BF16), moe_w_up[0].astype(BF16),
                     moe_w_down[0].astype(BF16), tm=tm_moe)
    x2 = _moe_combine(dest0, dest1, y, route, x2, row(norm_ffn_post[1]), gate, seq)
    return x2.reshape(bsz, seq, d)
```

```python
import functools
import math

import jax
import jax.numpy as jnp
from jax import lax
from jax.experimental import pallas as pl
from jax.experimental.pallas import tpu as pltpu

F32 = jnp.float32
BF16 = jnp.bfloat16

LANES = 128
SUBLANES = 8
VMEM_LIMIT = 56 * 1024 * 1024
RMS_EPS = 1e-6

S5_GROUP = 16
S5_STATE = 64
CHUNK = 16
OCT = LANES // S5_GROUP
OCT_STATE = 2 * OCT * S5_STATE
N_HEADS = 16
N_EXPERTS = 8
HIGHEST = lax.Precision.HIGHEST


def _cparams(*semantics):
    return pltpu.CompilerParams(dimension_semantics=semantics, vmem_limit_bytes=VMEM_LIMIT)


def _dot(a, b):
    return jnp.dot(a, b, preferred_element_type=F32)


def _rms(xf):
    ms = jnp.mean(xf * xf, axis=-1, keepdims=True)
    return xf * lax.rsqrt(ms + RMS_EPS)


def _norm_mod(xf, g, shift, scale):
    return _rms(xf) * (g * (1.0 + scale)) + shift


def _ada_kernel(c_ref, w_ref, b_ref, o_ref):
    s = c_ref[...]
    s = s * jax.nn.sigmoid(s)
    o_ref[...] = _dot(s.astype(BF16), w_ref[...].astype(BF16)) + b_ref[...]


def _ada(c_pad, w, b, tn=1024):
    nl, d, n = w.shape
    return pl.pallas_call(
        _ada_kernel,
        grid=(nl, n // tn),
        in_specs=[pl.BlockSpec((SUBLANES, d), lambda l, j: (0, 0)),
                  pl.BlockSpec((None, d, tn), lambda l, j: (l, 0, j)),
                  pl.BlockSpec((None, 1, tn), lambda l, j: (l, 0, j))],
        out_specs=pl.BlockSpec((None, SUBLANES, tn), lambda l, j: (l, 0, j)),
        out_shape=jax.ShapeDtypeStruct((nl, SUBLANES, n), F32),
        compiler_params=_cparams("parallel", "parallel"),
        name="ada_mods",
    )(c_pad, w, b.reshape(nl, 1, n))


def _expand_kernel(x_ref, rep_ref, o_ref, *, row_shift, col_shift):
    nb, rows, k = x_ref.shape
    n = o_ref.shape[2]
    x = x_ref[...].reshape(nb * rows, k).astype(BF16)
    spread = _dot(x, rep_ref[...])
    group = lax.shift_right_logical(lax.broadcasted_iota(jnp.int32, (nb * rows, n), 0), row_shift) & (OCT - 1)
    slot = lax.shift_right_logical(lax.broadcasted_iota(jnp.int32, (nb * rows, n), 1), col_shift) & (OCT - 1)
    o_ref[...] = jnp.where(group == slot, spread, 0.0).astype(BF16).reshape(nb, rows, n)


def _group_expand(x, rows_per_group, inner, nb):
    b, rows, k = x.shape
    n = k * OCT
    kk = jnp.arange(k, dtype=jnp.int32)
    nn = jnp.arange(n, dtype=jnp.int32)
    rep = jnp.logical_and((kk // inner)[:, None] == (nn // (inner * OCT))[None, :],
                          (kk % inner)[:, None] == (nn % inner)[None, :]).astype(BF16)
    return pl.pallas_call(
        functools.partial(_expand_kernel, row_shift=rows_per_group.bit_length() - 1,
                          col_shift=inner.bit_length() - 1),
        grid=(b // nb,),
        in_specs=[pl.BlockSpec((nb, rows, k), lambda i: (i, 0, 0)),
                  pl.BlockSpec((k, n), lambda i: (0, 0))],
        out_specs=pl.BlockSpec((nb, rows, n), lambda i: (i, 0, 0)),
        out_shape=jax.ShapeDtypeStruct((b, rows, n), BF16),
        compiler_params=_cparams("parallel"),
        name="s5_table_expand",
    )(x, rep)


def _s5_tables(a_re, a_im, b_re, b_im, c_re, c_im, log_step, d_skip, batch):
    g, p = a_re.shape
    h, lc = S5_GROUP, CHUNK
    no = g // OCT
    dt = jnp.exp(log_step)[:, None]
    lr, li = a_re * dt, a_im * dt
    k = jnp.arange(lc + 1, dtype=F32)[:, None, None]
    mag = jnp.exp(k * lr[None])
    pw_r, pw_i = mag * jnp.cos(k * li[None]), mag * jnp.sin(k * li[None])
    nr, ni = pw_r[1] - 1.0, pw_i[1]
    den = a_re * a_re + a_im * a_im
    qr, qi = (nr * a_re + ni * a_im) / den, (ni * a_re - nr * a_im) / den
    bbr = qr[..., None] * b_re - qi[..., None] * b_im
    bbi = qr[..., None] * b_im + qi[..., None] * b_re
    same_group = jnp.eye(OCT, dtype=F32).reshape(1, 1, OCT, 1, 1, OCT, 1)

    wr, wi = pw_r[:lc, :, None, :], pw_i[:lc, :, None, :]
    cwr = c_re[None] * wr - c_im[None] * wi
    cwi = c_re[None] * wi + c_im[None] * wr
    kk = jnp.einsum('tgop,gpi->tgio', jnp.concatenate([cwr, -cwi], axis=-1),
                    jnp.concatenate([bbr, bbi], axis=1), precision=HIGHEST)
    kk = kk.reshape(lc, no, OCT, h, h).transpose(1, 0, 2, 3, 4)
    kd = (kk[:, :, :, :, None, None, :] * same_group).reshape(no, lc, LANES, LANES)
    kd = jnp.concatenate([jnp.zeros((no, 1, LANES, LANES), F32), kd], axis=1)
    tiles = []
    for delta in range(lc // 2 - 1, -1, -1):
        top = jnp.concatenate([kd[:, 2 * delta + 1], kd[:, 2 * delta + 2]], axis=-1)
        bottom = jnp.concatenate([kd[:, 2 * delta], kd[:, 2 * delta + 1]], axis=-1)
        tiles += [top, bottom]
    m1 = jnp.concatenate(tiles, axis=-2)

    er = c_re[None] * pw_r[1:, :, None, :] - c_im[None] * pw_i[1:, :, None, :]
    ei = c_re[None] * pw_i[1:, :, None, :] + c_im[None] * pw_r[1:, :, None, :]
    e2 = jnp.stack([er, -ei]).reshape(2, lc, no, OCT, h, p).transpose(2, 0, 3, 5, 1, 4)
    m2 = _group_expand(e2.reshape(no * 2, OCT * p, lc * h), p, h, nb=1).reshape(no, OCT_STATE, lc * LANES)

    fr = pw_r[lc - 1::-1][:lc, :, :, None] * bbr[None] - pw_i[lc - 1::-1][:lc, :, :, None] * bbi[None]
    fi = pw_r[lc - 1::-1][:lc, :, :, None] * bbi[None] + pw_i[lc - 1::-1][:lc, :, :, None] * bbr[None]
    f2 = jnp.stack([fr, fi]).reshape(2, lc, no, OCT, p, h).transpose(2, 1, 3, 5, 0, 4)
    m3 = _group_expand(f2.reshape(no * lc, OCT * h, 2 * p), h, p, nb=8).reshape(no, lc * LANES, OCT_STATE)

    ar = pw_r[lc].reshape(no, OCT * p)
    ai = pw_i[lc].reshape(no, OCT * p)
    a1 = jnp.tile(jnp.concatenate([ar, ar], axis=-1), (1, batch))[:, None, :]
    a2 = jnp.tile(jnp.concatenate([-ai, ai], axis=-1), (1, batch))[:, None, :]
    dsk = jnp.tile(d_skip.reshape(no, 1, LANES), (1, 1, lc))
    return m1.astype(BF16), m2.astype(BF16), m3.astype(BF16), a1, a2, dsk


def _s5_pre_kernel(x_ref, g_ref, sh_ref, sc_ref, o_ref, slab):
    tm = x_ref.shape[0]
    nt = x_ref.shape[1] // LANES
    rows = tm // CHUNK
    h = _norm_mod(x_ref[...], g_ref[...], sh_ref[...], sc_ref[...])
    for j in range(nt):
        slab[j] = h[:, j * LANES:(j + 1) * LANES]
    for j in range(nt):
        for t in range(CHUNK):
            o_ref[j, :, t * LANES:(t + 1) * LANES] = slab.at[j][pl.ds(t, rows, stride=CHUNK), :].astype(BF16)


def _s5_pre(x2, g, shift, scale, seq, tm=512):
    t, d = x2.shape
    nt = d // LANES
    per_b = seq // tm
    return pl.pallas_call(
        _s5_pre_kernel,
        grid=(t // tm,),
        in_specs=[pl.BlockSpec((tm, d), lambda i: (i, 0)),
                  pl.BlockSpec((1, d), lambda i: (0, 0)),
                  pl.BlockSpec((None, 1, d), lambda i: (i // per_b, 0, 0)),
                  pl.BlockSpec((None, 1, d), lambda i: (i // per_b, 0, 0))],
        out_specs=pl.BlockSpec((nt, tm // CHUNK, CHUNK * LANES), lambda i: (0, i, 0)),
        out_shape=jax.ShapeDtypeStruct((nt, t // CHUNK, CHUNK * LANES), BF16),
        scratch_shapes=[pltpu.VMEM((nt, tm, LANES), F32)],
        compiler_params=_cparams("parallel"),
        name="s5_pre",
    )(x2, g, shift, scale)


def _s5_state_kernel(h_ref, m3_ref, o_ref):
    o_ref[...] = _dot(h_ref[...], m3_ref[...])


def _s5_state(ho, m3, batch):
    no, r, kc = ho.shape
    c = r // batch
    return pl.pallas_call(
        _s5_state_kernel,
        grid=(no, batch),
        in_specs=[pl.BlockSpec((None, c, kc), lambda j, b: (j, b, 0)),
                  pl.BlockSpec((None, kc, OCT_STATE), lambda j, b: (j, 0, 0))],
        out_specs=pl.BlockSpec((None, c, OCT_STATE), lambda j, b: (j, 0, b)),
        out_shape=jax.ShapeDtypeStruct((no, c, batch * OCT_STATE), F32),
        compiler_params=_cparams("parallel", "parallel"),
        name="s5_state",
    )(ho, m3)


def _s5_scan_kernel(v_ref, a1_ref, a2_ref, o_ref, *, batch, unroll):
    a1 = a1_ref[...]
    a2 = a2_ref[...]
    half = OCT_STATE // 2

    def swap(s):
        parts = []
        for b in range(batch):
            parts.append(s[:, b * OCT_STATE + half:(b + 1) * OCT_STATE])
            parts.append(s[:, b * OCT_STATE:b * OCT_STATE + half])
        return jnp.concatenate(parts, axis=-1)

    def body(i, s):
        base = pl.multiple_of(i * unroll, unroll)
        v = v_ref[pl.ds(base, unroll), :]
        rows = []
        for k in range(unroll):
            rows.append(s)
            s = a1 * s + a2 * swap(s) + v[k:k + 1, :]
        o_ref[pl.ds(base, unroll), :] = jnp.concatenate(rows, axis=0).astype(o_ref.dtype)
        return s

    lax.fori_loop(0, v_ref.shape[0] // unroll, body, jnp.zeros((1, v_ref.shape[1]), F32))


def _s5_scan(v, a1, a2, batch, unroll=16):
    no, c, w = v.shape
    return pl.pallas_call(
        functools.partial(_s5_scan_kernel, batch=batch, unroll=unroll),
        grid=(no,),
        in_specs=[pl.BlockSpec((None, c, w), lambda j: (j, 0, 0)),
                  pl.BlockSpec((None, 1, w), lambda j: (j, 0, 0)),
                  pl.BlockSpec((None, 1, w), lambda j: (j, 0, 0))],
        out_specs=pl.BlockSpec((None, c, w), lambda j: (j, 0, 0)),
        out_shape=jax.ShapeDtypeStruct((no, c, w), BF16),
        compiler_params=_cparams("parallel"),
        name="s5_scan",
    )(v, a1, a2)


def _s5_out_kernel(h_ref, s_ref, m1_ref, m2_ref, d_ref, o_ref):
    tw = m1_ref.shape[1]
    n_tiles = h_ref.shape[1] // tw
    for tile in range(n_tiles):
        cols = slice(tile * tw, (tile + 1) * tw)
        y = (_dot(h_ref[:, :(tile + 1) * tw], m1_ref[(n_tiles - 1 - tile) * tw:, :])
             + _dot(s_ref[...], m2_ref[:, cols])
             + d_ref[:, cols] * h_ref[:, cols].astype(F32))
        o_ref[:, cols] = jax.nn.gelu(y).astype(BF16)


def _s5_out(ho, sin, m1, m2, dsk, batch):
    no, r, kc = ho.shape
    c = r // batch
    return pl.pallas_call(
        _s5_out_kernel,
        grid=(no, batch),
        in_specs=[pl.BlockSpec((None, c, kc), lambda j, b: (j, b, 0)),
                  pl.BlockSpec((None, c, OCT_STATE), lambda j, b: (j, 0, b)),
                  pl.BlockSpec((None, kc, m1.shape[2]), lambda j, b: (j, 0, 0)),
                  pl.BlockSpec((None, OCT_STATE, kc), lambda j, b: (j, 0, 0)),
                  pl.BlockSpec((None, 1, kc), lambda j, b: (j, 0, 0))],
        out_specs=pl.BlockSpec((None, c, kc), lambda j, b: (j, b, 0)),
        out_shape=jax.ShapeDtypeStruct((no, r, kc), BF16),
        compiler_params=_cparams("parallel", "parallel"),
        name="s5_out",
    )(ho, sin, m1, m2, dsk)


def _s5_glu_kernel(y_ref, x_ref, w_ref, b_ref, g_ref, gate_ref, o_ref, ybuf, slab):
    nt, rows, _ = y_ref.shape
    for t in range(CHUNK):
        for j in range(nt):
            ybuf[t * rows:(t + 1) * rows, j * LANES:(j + 1) * LANES] = y_ref[j, :, t * LANES:(t + 1) * LANES]
    y = ybuf[...]
    z = _dot(y, w_ref[...]) + b_ref[...]
    m = y.astype(F32) * jax.nn.sigmoid(z)
    res = (1.0 + gate_ref[...]) * (_rms(m) * g_ref[...])
    for j in range(nt):
        for t in range(CHUNK):
            slab.at[j][pl.ds(t, rows, stride=CHUNK), :] = res[t * rows:(t + 1) * rows, j * LANES:(j + 1) * LANES]
    for j in range(nt):
        o_ref[:, j * LANES:(j + 1) * LANES] = x_ref[:, j * LANES:(j + 1) * LANES] + slab[j]


def _s5_glu(yo, x2, w, b, g, gate, seq, tm=512):
    t, d = x2.shape
    nt = d // LANES
    per_b = seq // tm
    return pl.pallas_call(
        _s5_glu_kernel,
        grid=(t // tm,),
        in_specs=[pl.BlockSpec((nt, tm // CHUNK, CHUNK * LANES), lambda i: (0, i, 0)),
                  pl.BlockSpec((tm, d), lambda i: (i, 0)),
                  pl.BlockSpec((d, d), lambda i: (0, 0)),
                  pl.BlockSpec((1, d), lambda i: (0, 0)),
                  pl.BlockSpec((1, d), lambda i: (0, 0)),
                  pl.BlockSpec((None, 1, d), lambda i: (i // per_b, 0, 0))],
        out_specs=pl.BlockSpec((tm, d), lambda i: (i, 0)),
        out_shape=jax.ShapeDtypeStruct((t, d), F32),
        scratch_shapes=[pltpu.VMEM((tm, d), BF16), pltpu.VMEM((nt, tm, LANES), F32)],
        compiler_params=_cparams("parallel"),
        name="s5_glu",
    )(yo, x2, w, b, g, gate)


FFN_TF = 1024
MOE_TM = 512


def _swiglu_step(h, wg_ref, wu_ref, wd_ref, hid_scr):
    for c in range(hid_scr.shape[1] // LANES):
        cols = slice(c * LANES, (c + 1) * LANES)
        r = _dot(h, jnp.concatenate([wg_ref[:, cols], wu_ref[:, cols]], axis=1))
        a, u = r[:, :LANES], r[:, LANES:]
        hid_scr[:, cols] = (a * jax.nn.sigmoid(a) * u).astype(BF16)
    return _dot(hid_scr[...], wd_ref[...])


def _ffn_kernel(x_ref, g_ref, sh_ref, sc_ref, wg_ref, wu_ref, wd_ref, gp_ref, gate_ref, o_ref, h_scr, hid_scr):
    j = pl.program_id(1)

    @pl.when(j == 0)
    def _():
        h_scr[...] = _norm_mod(x_ref[...], g_ref[...], sh_ref[...], sc_ref[...]).astype(BF16)
        o_ref[...] = jnp.zeros_like(o_ref)

    o_ref[...] += _swiglu_step(h_scr[...], wg_ref, wu_ref, wd_ref, hid_scr)

    @pl.when(j == pl.num_programs(1) - 1)
    def _():
        o_ref[...] = x_ref[...] + (1.0 + gate_ref[...]) * (_rms(o_ref[...]) * gp_ref[...])


def _ffn(x2, g, shift, scale, wg, wu, wd, gp, gate, seq, tm=512):
    t, d = x2.shape
    f = wd.shape[0]
    per_b = seq // tm
    mod = pl.BlockSpec((None, 1, d), lambda i, j: (i // per_b, 0, 0))
    row = pl.BlockSpec((1, d), lambda i, j: (0, 0))
    wcol = pl.BlockSpec((d, FFN_TF), lambda i, j: (0, j))
    return pl.pallas_call(
        _ffn_kernel,
        grid=(t // tm, f // FFN_TF),
        in_specs=[pl.BlockSpec((tm, d), lambda i, j: (i, 0)), row, mod, mod, wcol, wcol,
                  pl.BlockSpec((FFN_TF, d), lambda i, j: (j, 0)),
                  row, mod],
        out_specs=pl.BlockSpec((tm, d), lambda i, j: (i, 0)),
        out_shape=jax.ShapeDtypeStruct((t, d), F32),
        scratch_shapes=[pltpu.VMEM((tm, d), BF16), pltpu.VMEM((tm, FFN_TF), BF16)],
        compiler_params=_cparams("parallel", "arbitrary"),
        name="dense_ffn",
    )(x2, g, shift, scale, wg, wu, wd, gp, gate)


def _qkv_kernel(x_ref, gq_ref, shq_ref, scq_ref, gkv_ref, shkv_ref, sckv_ref, w_ref, o_ref, hq, hkv, *, qscale):
    n = pl.program_id(1)

    @pl.when(n == 0)
    def _():
        xf = x_ref[...]
        xn = _rms(xf)
        hq[...] = (xn * (gq_ref[...] * (1.0 + scq_ref[...])) + shq_ref[...]).astype(BF16)
        hkv[...] = (xn * (gkv_ref[...] * (1.0 + sckv_ref[...])) + shkv_ref[...]).astype(BF16)
        o_ref[...] = (_dot(hq[...], w_ref[...]) * qscale).astype(BF16)

    @pl.when(n > 0)
    def _():
        o_ref[...] = _dot(hkv[...], w_ref[...]).astype(BF16)


def _qkv(x2, gq, shq, scq, gkv, shkv, sckv, w3, seq, tm=512):
    t, d = x2.shape
    per_b = seq // tm
    mod = pl.BlockSpec((None, 1, d), lambda i, n: (i // per_b, 0, 0))
    row = pl.BlockSpec((1, d), lambda i, n: (0, 0))
    qscale = math.log2(math.e) / math.sqrt(d // N_HEADS)
    return pl.pallas_call(
        functools.partial(_qkv_kernel, qscale=qscale),
        grid=(t // tm, 3),
        in_specs=[pl.BlockSpec((tm, d), lambda i, n: (i, 0)), row, mod, mod, row, mod, mod,
                  pl.BlockSpec((None, d, d), lambda i, n: (n, 0, 0))],
        out_specs=pl.BlockSpec((None, tm, d), lambda i, n: (n, i, 0)),
        out_shape=jax.ShapeDtypeStruct((3, t, d), BF16),
        scratch_shapes=[pltpu.VMEM((tm, d), BF16), pltpu.VMEM((tm, d), BF16)],
        compiler_params=_cparams("parallel", "arbitrary"),
        name="qkv_proj",
    )(x2, gq, shq, scq, gkv, shkv, sckv, w3)


ATTN_TK = 256
EXP2_ZERO_BELOW = -150.0
NO_WEIGHT = 1e30


def _attn_kernel(q_ref, k_ref, v_ref, o_ref, *, tb, tk, nh, hd):
    qi = pl.program_id(2)
    causal = (lax.broadcasted_iota(jnp.int32, (tb, tk), 1)
              < lax.broadcasted_iota(jnp.int32, (tb, tk), 0))
    krow = lax.broadcasted_iota(jnp.int32, (tk, tk), 0)
    kcol = lax.broadcasted_iota(jnp.int32, (tk, tk), 1)
    later = jnp.where(krow > kcol, 1.0, 0.0).astype(BF16)
    later2 = jnp.concatenate([later, later], axis=0)
    sign_bit = jnp.uint32(0x80000000)

    heads = range(nh)
    lanes = [slice(g * hd, (g + 1) * hd) for g in heads]

    def walk(spans, accs, runs):
        pairs = [(b, g) for b in range(len(spans)) for g in heads]
        start = [pl.multiple_of(s, tk) for s, _, _ in spans]
        y, logb, sm, suffix = {}, {}, {}, {}
        for b, g in pairs:
            y[b, g] = lax.dot_general(q_ref[:, lanes[g]], k_ref[pl.ds(start[b], tk), lanes[g]],
                                      (((1,), (1,)), ((), ())), preferred_element_type=F32)
        for b, g in pairs:
            neg_abs = lax.bitcast_convert_type(lax.bitcast_convert_type(y[b, g], jnp.uint32) | sign_bit, F32)
            s = jnp.maximum(y[b, g], 0.0) + jnp.log2(1.0 + jnp.exp2(neg_abs))
            logb[b, g] = y[b, g] - s
            sm[b, g] = jnp.where(causal, s, 0.0) if spans[b][1] else s
        for b, g in pairs:
            hi = sm[b, g].astype(BF16)
            lo = (sm[b, g] - hi.astype(F32)).astype(BF16)
            suffix[b, g] = _dot(jnp.concatenate([hi, lo], axis=1), later2)
        new_accs, new_runs = [], []
        for g in heads:
            run, acc = runs[g], accs[g]
            for b, (_, diagonal, exists) in enumerate(spans):
                if exists is not True:
                    run = jnp.where(exists, run, NO_WEIGHT)
                w = jnp.exp2((logb[b, g] - suffix[b, g]) - run)
                if diagonal:
                    w = jnp.where(causal, w, 0.0)
                acc = acc + _dot(w.astype(BF16), v_ref[pl.ds(start[b], tk), lanes[g]])
                run = run + suffix[b, g][:, 0:1] + sm[b, g][:, 0:1]
            new_accs.append(acc)
            new_runs.append(run)
        return tuple(new_accs), tuple(new_runs)

    accs = tuple(jnp.zeros((tb, hd), F32) for _ in heads)
    runs = tuple(jnp.zeros((tb, 1), F32) for _ in heads)
    start_a = qi * tb
    start_b = jnp.maximum(start_a - tk, 0)
    accs, runs = walk([(start_a, True, True), (start_b, False, qi > 0)], accs, runs)

    def cond(c):
        left, _, runs = c
        low = runs[0]
        for g in range(1, nh):
            low = jnp.minimum(low, runs[g])
        return jnp.logical_and(left > 0, jnp.min(low) <= -EXP2_ZERO_BELOW)

    def body(c):
        left, accs, runs = c
        start = left - tk
        accs, runs = walk([(start, False, True)], accs, runs)
        return start, accs, runs

    _, accs, _ = lax.while_loop(cond, body, (start_b, accs, runs))
    for g in range(nh):
        o_ref[:, g * hd:(g + 1) * hd] = accs[g].astype(o_ref.dtype)


def _attention(qkv, batch, seq, tb=256, tk=ATTN_TK, nh=4):
    _, t, d = qkv.shape
    assert tb == tk, "query tiles and key spans must have the same size"
    hd = d // N_HEADS
    nq = seq // tb
    return pl.pallas_call(
        functools.partial(_attn_kernel, tb=tb, tk=tk, nh=nh, hd=hd),
        grid=(batch, N_HEADS // nh, nq),
        in_specs=[pl.BlockSpec((None, tb, nh * hd), lambda b, h, i: (0, b * nq + i, h)),
                  pl.BlockSpec((None, seq, nh * hd), lambda b, h, i: (1, b, h)),
                  pl.BlockSpec((None, seq, nh * hd), lambda b, h, i: (2, b, h))],
        out_specs=pl.BlockSpec((tb, nh * hd), lambda b, h, i: (b * nq + i, h)),
        out_shape=jax.ShapeDtypeStruct((t, d), BF16),
        compiler_params=_cparams("parallel", "parallel", "parallel"),
        name="stickbreak_attn",
    )(qkv, qkv, qkv)


def _oproj_kernel(a_ref, x_ref, w_ref, g_ref, gate_ref, o_ref):
    m = _dot(a_ref[...], w_ref[...])
    o_ref[...] = x_ref[...] + (1.0 + gate_ref[...]) * (_rms(m) * g_ref[...])


def _oproj(a, x2, w, g, gate, seq, tm=512):
    t, d = x2.shape
    per_b = seq // tm
    return pl.pallas_call(
        _oproj_kernel,
        grid=(t // tm,),
        in_specs=[pl.BlockSpec((tm, d), lambda i: (i, 0)),
                  pl.BlockSpec((tm, d), lambda i: (i, 0)),
                  pl.BlockSpec((d, d), lambda i: (0, 0)),
                  pl.BlockSpec((1, d), lambda i: (0, 0)),
                  pl.BlockSpec((None, 1, d), lambda i: (i // per_b, 0, 0))],
        out_specs=pl.BlockSpec((tm, d), lambda i: (i, 0)),
        out_shape=jax.ShapeDtypeStruct((t, d), F32),
        compiler_params=_cparams("parallel"),
        name="attn_out_proj",
    )(a, x2, w, g, gate)


def _router_kernel(x_ref, g_ref, sh_ref, sc_ref, wr_ref, h_ref, r_ref):
    h = _norm_mod(x_ref[...], g_ref[...], sh_ref[...], sc_ref[...])
    h_ref[...] = h
    logits = jnp.dot(h, wr_ref[...], preferred_element_type=F32, precision=HIGHEST)
    lane = lax.broadcasted_iota(jnp.int32, logits.shape, 1).astype(F32)
    neg = jnp.float32(-jnp.inf)
    lg = jnp.where(lane < N_EXPERTS, logits, neg)
    m1 = jnp.max(lg, axis=-1, keepdims=True)
    i1 = jnp.min(jnp.where(lg == m1, lane, float(LANES)), axis=-1, keepdims=True)
    lg2 = jnp.where(lane == i1, neg, lg)
    m2 = jnp.max(lg2, axis=-1, keepdims=True)
    i2 = jnp.min(jnp.where(lg2 == m2, lane, float(LANES)), axis=-1, keepdims=True)
    e = jnp.exp(m2 - m1)
    p1 = 1.0 / (1.0 + e)
    p2 = e * p1
    r_ref[...] = jnp.where(lane == 0, i1, jnp.where(lane == 1, i2, jnp.where(lane == 2, p1, jnp.where(lane == 3, p2, 0.0))))


def _router(x2, g, shift, scale, wr_pad, seq, tm=512):
    t, d = x2.shape
    per_b = seq // tm
    mod = pl.BlockSpec((None, 1, d), lambda i: (i // per_b, 0, 0))
    return pl.pallas_call(
        _router_kernel,
        grid=(t // tm,),
        in_specs=[pl.BlockSpec((tm, d), lambda i: (i, 0)),
                  pl.BlockSpec((1, d), lambda i: (0, 0)), mod, mod,
                  pl.BlockSpec((d, LANES), lambda i: (0, 0))],
        out_specs=[pl.BlockSpec((tm, d), lambda i: (i, 0)),
                   pl.BlockSpec((tm, LANES), lambda i: (i, 0))],
        out_shape=[jax.ShapeDtypeStruct((t, d), F32), jax.ShapeDtypeStruct((t, LANES), F32)],
        compiler_params=_cparams("parallel"),
        name="moe_router",
    )(x2, g, shift, scale, wr_pad)


def _row_copy(src_hbm, dst, sem, src_row, dst_row):
    return pltpu.make_async_copy(src_hbm.at[pl.ds(src_row, 1)], dst.at[pl.ds(dst_row, 1)], sem)


def _moe_kernel(te_ref, nu_ref, tok_ref, nxt_ref, h_hbm, wg_ref, wu_ref, wd_ref, o_ref, xbuf, xb16, hid_scr, sem):
    i = pl.program_id(0)
    j = pl.program_id(1)
    tm = xb16.shape[0]
    used = i < nu_ref[0]
    slot = lax.rem(i, 2)

    def gather(idx_ref, s):
        def start(r, c):
            _row_copy(h_hbm, xbuf.at[s], sem.at[s], idx_ref[0, r], r).start()
            return c
        lax.fori_loop(0, tm, start, 0, unroll=8)

    @pl.when(jnp.logical_and(i == 0, j == 0))
    def _():
        gather(tok_ref, 0)

    @pl.when(jnp.logical_and(i + 1 < nu_ref[0], j == 0))
    def _():
        gather(nxt_ref, 1 - slot)

    @pl.when(jnp.logical_and(used, j == 0))
    def _():
        def wait(r, c):
            _row_copy(h_hbm, xbuf.at[slot], sem.at[slot], 0, r).wait()
            return c
        lax.fori_loop(0, tm, wait, 0, unroll=8)
        xb16[...] = xbuf[slot].astype(BF16)

    @pl.when(j == 0)
    def _():
        o_ref[...] = jnp.zeros_like(o_ref)

    @pl.when(used)
    def _():
        o_ref[...] += _swiglu_step(xb16[...], wg_ref, wu_ref, wd_ref, hid_scr)


def _moe_experts(tile_expert, n_used, sorted_tok, h, wg, wu, wd, tm=MOE_TM):
    t, d = h.shape
    f = wd.shape[1]
    nt = sorted_tok.shape[0]
    nf = f // FFN_TF

    def wcol(i, j, te, nu):
        return (te[i], 0, jnp.where(i < nu[0], j, nf - 1))

    def wrow(i, j, te, nu):
        return (te[i], jnp.where(i < nu[0], j, nf - 1), 0)

    grid_spec = pltpu.PrefetchScalarGridSpec(
        num_scalar_prefetch=2,
        grid=(nt, nf),
        in_specs=[pl.BlockSpec((None, 1, tm), lambda i, j, te, nu: (i, 0, 0), memory_space=pltpu.SMEM),
                  pl.BlockSpec((None, 1, tm), lambda i, j, te, nu: (jnp.minimum(i + 1, nt - 1), 0, 0),
                               memory_space=pltpu.SMEM),
                  pl.BlockSpec(memory_space=pl.ANY),
                  pl.BlockSpec((None, d, FFN_TF), wcol),
                  pl.BlockSpec((None, d, FFN_TF), wcol),
                  pl.BlockSpec((None, FFN_TF, d), wrow)],
        out_specs=pl.BlockSpec((tm, d), lambda i, j, te, nu: (i, 0)),
        scratch_shapes=[pltpu.VMEM((2, tm, d), F32), pltpu.VMEM((tm, d), BF16), pltpu.VMEM((tm, FFN_TF), BF16),
                        pltpu.SemaphoreType.DMA((2,))],
    )
    return pl.pallas_call(
        _moe_kernel,
        grid_spec=grid_spec,
        out_shape=jax.ShapeDtypeStruct((nt * tm, d), F32),
        compiler_params=_cparams("arbitrary", "arbitrary"),
        name="moe_experts",
    )(tile_expert, n_used, sorted_tok, sorted_tok, h, wg, wu, wd)


def _combine_kernel(d0_ref, d1_ref, n0_ref, n1_ref, y_hbm, r_ref, x_ref, g_ref, gate_ref, o_ref, buf, sem):
    i = pl.program_id(0)
    tm = x_ref.shape[0]
    slot = lax.rem(i, 2)

    def gather(i0_ref, i1_ref, s):
        def start(r, c):
            _row_copy(y_hbm, buf.at[s, 0], sem.at[s], i0_ref[0, r], r).start(priority=0)
            _row_copy(y_hbm, buf.at[s, 1], sem.at[s], i1_ref[0, r], r).start(priority=1)
            return c
        lax.fori_loop(0, tm, start, 0, unroll=8)

    @pl.when(i == 0)
    def _():
        gather(d0_ref, d1_ref, 0)

    @pl.when(i + 1 < pl.num_programs(0))
    def _():
        gather(n0_ref, n1_ref, 1 - slot)

    def wait(r, c):
        _row_copy(y_hbm, buf.at[slot, 0], sem.at[slot], 0, r).wait()
        _row_copy(y_hbm, buf.at[slot, 1], sem.at[slot], 0, r).wait()
        return c

    lax.fori_loop(0, tm, wait, 0, unroll=8)
    r = r_ref[...]
    f = r[:, 2:3] * buf[slot, 0] + r[:, 3:4] * buf[slot, 1]
    o_ref[...] = x_ref[...] + (1.0 + gate_ref[...]) * (_rms(f) * g_ref[...])


def _moe_combine(dest0, dest1, y, route, x2, g, gate, seq, tm=256):
    t, d = x2.shape
    per_b = seq // tm
    nt = t // tm
    idx = pl.BlockSpec((None, 1, tm), lambda i: (i, 0, 0), memory_space=pltpu.SMEM)
    nxt = pl.BlockSpec((None, 1, tm), lambda i: (jnp.minimum(i + 1, nt - 1), 0, 0), memory_space=pltpu.SMEM)
    d0 = dest0.reshape(nt, 1, tm)
    d1 = dest1.reshape(nt, 1, tm)
    return pl.pallas_call(
        _combine_kernel,
        grid=(nt,),
        in_specs=[idx, idx, nxt, nxt,
                  pl.BlockSpec(memory_space=pl.ANY),
                  pl.BlockSpec((tm, LANES), lambda i: (i, 0)),
                  pl.BlockSpec((tm, d), lambda i: (i, 0)),
                  pl.BlockSpec((1, d), lambda i: (0, 0)),
                  pl.BlockSpec((None, 1, d), lambda i: (i // per_b, 0, 0))],
        out_specs=pl.BlockSpec((tm, d), lambda i: (i, 0)),
        out_shape=jax.ShapeDtypeStruct((t, d), F32),
        scratch_shapes=[pltpu.VMEM((2, 2, tm, d), F32), pltpu.SemaphoreType.DMA((2,))],
        compiler_params=_cparams("arbitrary"),
        name="moe_combine",
    )(d0, d1, d0, d1, y, route, x2, g, gate)


def _routing_plan(route, tm):
    t = route.shape[0]
    ex = route[:, 0:2].astype(jnp.int32).T.reshape(-1)
    onehot = (ex[:, None] == jnp.arange(N_EXPERTS, dtype=jnp.int32)[None, :]).astype(jnp.int32)
    csum = jnp.cumsum(onehot, axis=0)
    rank = jnp.sum(onehot * csum, axis=1) - 1
    counts = csum[-1]
    tiles = (counts + tm - 1) // tm
    tile_end = jnp.cumsum(tiles)
    offset = (tile_end - tiles) * tm
    dest = offset[ex] + rank
    nt = (2 * t) // tm + N_EXPERTS
    n_used = tile_end[-1]
    tile_ids = jnp.minimum(jnp.arange(nt, dtype=jnp.int32), n_used - 1)
    tile_expert = jnp.sum((tile_ids[:, None] >= tile_end[None, :]).astype(jnp.int32), axis=1)
    token = jnp.arange(2 * t, dtype=jnp.int32) % t
    sorted_tok = jnp.zeros((nt * tm,), jnp.int32).at[dest].set(token)
    return (tile_expert.astype(jnp.int32), n_used.reshape(1).astype(jnp.int32),
            sorted_tok.reshape(nt, 1, tm), dest[:t], dest[t:])


def kernel(x, c, ada_mix_w, ada_mix_b, norm_mix_pre, norm_mix_post, ada_ffn_w, ada_ffn_b, norm_ffn_pre, norm_ffn_post, s5_a_re, s5_a_im, s5_b_re, s5_b_im, s5_c_re, s5_c_im, s5_d, s5_log_step, s5_glu_w, s5_glu_b, kv_ada_w, kv_ada_b, kv_norm, w_k, w_v, w_q, w_o, ffn_w_gate, ffn_w_up, ffn_w_down, moe_w_router, moe_w_gate, moe_w_up, moe_w_down):
    bsz, seq, d = x.shape
    x2 = x.reshape(bsz * seq, d)

    assert bsz <= SUBLANES
    c_pad = jnp.zeros((SUBLANES, d), F32).at[:bsz].set(c)
    mix = _ada(c_pad, ada_mix_w, ada_mix_b)
    ffn = _ada(c_pad, ada_ffn_w, ada_ffn_b)
    kvm = _ada(c_pad, kv_ada_w[None], kv_ada_b[None])

    def mods(m, layer, n):
        return [m[layer, :bsz, k * d:(k + 1) * d].reshape(bsz, 1, d) for k in range(n)]

    def row(v):
        return v.reshape(1, d)

    shift, scale, gate = mods(mix, 0, 3)
    m1, m2, m3, a1, a2, dsk = _s5_tables(s5_a_re[0], s5_a_im[0], s5_b_re[0], s5_b_im[0],
                                         s5_c_re[0], s5_c_im[0], s5_log_step[0], s5_d[0], bsz)
    ho = _s5_pre(x2, row(norm_mix_pre[0]), shift, scale, seq)
    v = _s5_state(ho, m3, bsz)
    sin = _s5_scan(v, a1, a2, bsz)
    yo = _s5_out(ho, sin, m1, m2, dsk, bsz)
    x2 = _s5_glu(yo, x2, s5_glu_w[0].astype(BF16), row(s5_glu_b[0]), row(norm_mix_post[0]), gate, seq)

    shift, scale, gate = mods(ffn, 0, 3)
    x2 = _ffn(x2, row(norm_ffn_pre[0]), shift, scale, ffn_w_gate[0].astype(BF16), ffn_w_up[0].astype(BF16),
              ffn_w_down[0].astype(BF16), row(norm_ffn_post[0]), gate, seq)

    shift, scale, gate = mods(mix, 1, 3)
    kv_shift, kv_scale = mods(kvm, 0, 2)
    w3 = jnp.stack([w_q[0], w_k, w_v]).astype(BF16)
    qkv = _qkv(x2, row(norm_mix_pre[1]), shift, scale, row(kv_norm), kv_shift, kv_scale, w3, seq)
    att = _attention(qkv, bsz, seq)
    x2 = _oproj(att, x2, w_o[0].astype(BF16), row(norm_mix_post[1]), gate, seq)

    shift, scale, gate = mods(ffn, 1, 3)
    wr_pad = jnp.zeros((d, LANES), F32).at[:, :N_EXPERTS].set(moe_w_router[0])
    h, route = _router(x2, row(norm_ffn_pre[1]), shift, scale, wr_pad, seq)
    tm_moe = MOE_TM
    tile_expert, n_used, sorted_tok, dest0, dest1 = _routing_plan(route, tm_moe)
    y = _moe_experts(tile_expert, n_used, sorted_tok, h, moe_w_gate[0].astype(BF16), moe_w_up[0].astype(BF16),
                     moe_w_down[0].astype(BF16), tm=tm_moe)
    x2 = _moe_combine(dest0, dest1, y, route, x2, row(norm_ffn_post[1]), gate, seq)
    return x2.reshape(bsz, seq, d)
```
